```python
import math
import jax, jax.numpy as jnp
from jax import lax
import numpy as np

D_MODEL = 1024
BATCH = 4
SEQ = 8192
DEPTH = 2

CHUNK = 64
N_A_LAYERS = DEPTH // 2
N_B_LAYERS = DEPTH - N_A_LAYERS

RWKV_HEAD = 64
RWKV_HEADS = D_MODEL // RWKV_HEAD
DECAY_LORA = 64
A_LORA = 64
GATE_LORA = 160
LNX_EPS = 64e-5

DA_HEAD_DIM = 64
DA_HEADS = D_MODEL // (2 * DA_HEAD_DIM)
DA_V_DIM = 2 * DA_HEAD_DIM
ROPE_DIM = DA_HEAD_DIM // 4
ROPE_THETA = 500000.0
Q_BLOCK = 128
SUBLN_EPS = 1e-5

D_FF = ((8 * D_MODEL + 3 * 256 - 1) // (3 * 256)) * 256

NORM_EPS = 1e-6

kernel_name = "yoco_rwkv7_diffattn_trunk"


def rms_norm(x, g, eps=NORM_EPS):
    xf = x.astype(jnp.float32)
    y = xf * lax.rsqrt(jnp.mean(xf * xf, axis=-1, keepdims=True) + eps)
    return (y * g.astype(jnp.float32)).astype(x.dtype)


def swiglu_ffn(h, w_gu, w_down):
    gate, up = jnp.split(h @ w_gu, 2, axis=-1)
    return (jax.nn.silu(gate) * up) @ w_down


def partial_rope(x, pos):
    half = ROPE_DIM // 2
    inv = jnp.power(ROPE_THETA, -jnp.arange(0, ROPE_DIM, 2, dtype=jnp.float32) / ROPE_DIM)
    ang = pos[:, None] * inv[None, :]
    cos = jnp.cos(ang)[None, :, None, :]
    sin = jnp.sin(ang)[None, :, None, :]
    xf = x.astype(jnp.float32)
    x1 = xf[..., :half]
    x2 = xf[..., half:ROPE_DIM]
    out = jnp.concatenate([x1 * cos - x2 * sin, x2 * cos + x1 * sin, xf[..., ROPE_DIM:]], axis=-1)
    return out.astype(x.dtype)


def rwkv7_time_mix(h, mu, w_r, w_k, w_v, w_o, w0, w1, w2, a0, a1, a2, g1, g2,
                   k_k, k_a, r_k, lnx_w, lnx_b):
    B, S, D = h.shape
    H, N = RWKV_HEADS, RWKV_HEAD
    h_prev = jnp.pad(h, ((0, 0), (1, 0), (0, 0)))[:, :-1]
    hh = h_prev - h
    xr = h + hh * mu[0]
    xw = h + hh * mu[1]
    xk = h + hh * mu[2]
    xv = h + hh * mu[3]
    xa = h + hh * mu[4]
    xg = h + hh * mu[5]

    r = xr @ w_r
    w = -jax.nn.softplus(-(w0 + jnp.tanh(xw @ w1) @ w2)) - 0.5
    k = xk @ w_k
    v = xv @ w_v
    a = jax.nn.sigmoid(a0 + (xa @ a1) @ a2)
    g = jax.nn.sigmoid(xg @ g1) @ g2

    kk = (k * k_k).astype(jnp.float32).reshape(B, S, H, N)
    kk = kk / jnp.maximum(jnp.sqrt(jnp.sum(kk * kk, axis=-1, keepdims=True)), 1e-12)
    k = k * (1.0 + (a - 1.0) * k_a)

    def heads(t):
        return t.astype(jnp.float32).reshape(B, S, H, N)

    r_h, k_h, v_h, a_h = heads(r), heads(k), heads(v), heads(a)
    decay = jnp.exp(-jnp.exp(heads(w)))
    a_vec = -kk
    b_vec = kk * a_h

    def time_major(t):
        return t.transpose(1, 0, 2, 3)

    def step(state, inp):
        r_t, w_t, k_t, v_t, a_t, b_t = inp
        sa = jnp.einsum('bhvk,bhk->bhv', state, a_t)
        state = (state * w_t[:, :, None, :]
                 + sa[..., None] * b_t[:, :, None, :]
                 + v_t[..., None] * k_t[:, :, None, :])
        y_t = jnp.einsum('bhvk,bhk->bhv', state, r_t)
        return state, y_t

    state0 = jnp.zeros((B, H, N, N), jnp.float32)
    _, y = lax.scan(step, state0, (time_major(r_h), time_major(decay), time_major(k_h),
                                   time_major(v_h), time_major(a_vec), time_major(b_vec)))
    y = y.transpose(1, 0, 2, 3)

    mean = jnp.mean(y, axis=-1, keepdims=True)
    var = jnp.mean(jnp.square(y - mean), axis=-1, keepdims=True)
    y = ((y - mean) * lax.rsqrt(var + LNX_EPS)).reshape(B, S, D)
    y = y * lnx_w.astype(jnp.float32) + lnx_b.astype(jnp.float32)
    bonus = jnp.sum(r_h * k_h * r_k.astype(jnp.float32), axis=-1, keepdims=True) * v_h
    y = y + bonus.reshape(B, S, D)
    return (y * g.astype(jnp.float32)).astype(h.dtype) @ w_o


def shared_kv(x, kv_g, kv_w, k_norm_g, pos):
    B, S, _ = x.shape
    hkv = rms_norm(x, kv_g)
    k, v = jnp.split(hkv @ kv_w, 2, axis=-1)
    k = k.reshape(B, S, 2 * DA_HEADS, DA_HEAD_DIM)
    k = partial_rope(rms_norm(k, k_norm_g), pos)
    v = v.reshape(B, S, DA_HEADS, DA_V_DIM)
    return k, v


def diff_attention(h, k, v, pos, w_q, q_norm_g, lam_q1, lam_k1, lam_q2, lam_k2,
                   subln_g, w_o, lam_init):
    B, S, D = h.shape
    H = DA_HEADS
    q = (h @ w_q).reshape(B, S, 2 * H, DA_HEAD_DIM)
    q = partial_rope(rms_norm(q, q_norm_g), pos)
    f32 = jnp.float32
    lam = (jnp.exp(jnp.sum(lam_q1.astype(f32) * lam_k1.astype(f32)))
           - jnp.exp(jnp.sum(lam_q2.astype(f32) * lam_k2.astype(f32))) + lam_init)
    scale = DA_HEAD_DIM ** -0.5
    k_chunk = jnp.arange(S) // CHUNK
    n_blocks = S // Q_BLOCK

    def block(i):
        start = i * Q_BLOCK
        q_blk = lax.dynamic_slice_in_dim(q, start, Q_BLOCK, axis=1)
        s = jnp.einsum('bqhd,bkhd->bhqk', q_blk, k).astype(f32) * scale
        q_chunk = (start + jnp.arange(Q_BLOCK)) // CHUNK
        allowed = k_chunk[None, :] <= q_chunk[:, None]
        s = jnp.where(allowed[None, None], s, -jnp.inf)
        p = jax.nn.softmax(s, axis=-1).reshape(B, H, 2, Q_BLOCK, S)
        p = p[:, :, 0] - lam * p[:, :, 1]
        return jnp.einsum('bhqk,bkhe->bqhe', p.astype(v.dtype), v)

    o = lax.map(block, jnp.arange(n_blocks))
    o = o.transpose(1, 0, 2, 3, 4).reshape(B, S, H, DA_V_DIM)
    o = rms_norm(o, subln_g, eps=SUBLN_EPS) * (1.0 - lam_init)
    return o.reshape(B, S, D) @ w_o


def setup_inputs(seed: int = 0) -> dict:
    key = jax.random.key(seed)
    ks = iter(jax.random.split(key, 48))
    D, F, NA, NB = D_MODEL, D_FF, N_A_LAYERS, N_B_LAYERS
    nrm = lambda shape, s: jax.random.normal(next(ks), shape, jnp.float32) * s
    gain = lambda shape: 1.0 + 0.02 * jax.random.normal(next(ks), shape, jnp.float32)
    inp = {}
    inp["x"] = nrm((BATCH, SEQ, D), 1.0)
    inp["g_mix"] = gain((DEPTH, D))
    inp["g_ffn"] = gain((DEPTH, D))
    inp["rw_mu"] = jax.random.uniform(next(ks), (NA, 6, D), jnp.float32)
    inp["rw_w_r"] = nrm((NA, D, D), D ** -0.5)
    inp["rw_w_k"] = nrm((NA, D, D), D ** -0.5)
    inp["rw_w_v"] = nrm((NA, D, D), D ** -0.5)
    inp["rw_w_o"] = nrm((NA, D, D), D ** -0.5)
    inp["rw_w0"] = jax.random.uniform(next(ks), (NA, D), jnp.float32, -6.0, -1.0)
    inp["rw_w1"] = nrm((NA, D, DECAY_LORA), D ** -0.5)
    inp["rw_w2"] = nrm((NA, DECAY_LORA, D), 0.5 * DECAY_LORA ** -0.5)
    inp["rw_a0"] = nrm((NA, D), 0.1)
    inp["rw_a1"] = nrm((NA, D, A_LORA), D ** -0.5)
    inp["rw_a2"] = nrm((NA, A_LORA, D), A_LORA ** -0.5)
    inp["rw_g1"] = nrm((NA, D, GATE_LORA), D ** -0.5)
    inp["rw_g2"] = nrm((NA, GATE_LORA, D), GATE_LORA ** -0.5)
    inp["rw_k_k"] = 0.85 + nrm((NA, D), 0.05)
    inp["rw_k_a"] = 1.0 + nrm((NA, D), 0.05)
    inp["rw_r_k"] = nrm((NA, RWKV_HEADS, RWKV_HEAD), 0.1)
    inp["rw_lnx_w"] = gain((NA, D))
    inp["rw_lnx_b"] = nrm((NA, D), 0.02)
    inp["kv_g"] = gain((D,))
    inp["kv_w"] = nrm((D, 2 * D), D ** -0.5)
    inp["k_norm_g"] = gain((DA_HEAD_DIM,))
    inp["da_w_q"] = nrm((NB, D, D), D ** -0.5)
    inp["da_q_norm_g"] = gain((NB, DA_HEAD_DIM))
    inp["da_lam_q1"] = nrm((NB, DA_HEAD_DIM), 0.1)
    inp["da_lam_k1"] = nrm((NB, DA_HEAD_DIM), 0.1)
    inp["da_lam_q2"] = nrm((NB, DA_HEAD_DIM), 0.1)
    inp["da_lam_k2"] = nrm((NB, DA_HEAD_DIM), 0.1)
    inp["da_subln_g"] = gain((NB, DA_V_DIM))
    inp["da_w_o"] = nrm((NB, D, D), D ** -0.5)
    inp["ffn_w_gu"] = nrm((DEPTH, D, 2 * F), D ** -0.5)
    inp["ffn_w_down"] = nrm((DEPTH, F, D), F ** -0.5)
    return inp


def reference(x, g_mix, g_ffn, rw_mu, rw_w_r, rw_w_k, rw_w_v, rw_w_o, rw_w0, rw_w1, rw_w2,
              rw_a0, rw_a1, rw_a2, rw_g1, rw_g2, rw_k_k, rw_k_a, rw_r_k, rw_lnx_w, rw_lnx_b,
              kv_g, kv_w, k_norm_g, da_w_q, da_q_norm_g, da_lam_q1, da_lam_k1, da_lam_q2,
              da_lam_k2, da_subln_g, da_w_o, ffn_w_gu, ffn_w_down):
    S = x.shape[1]
    pos = jnp.arange(S, dtype=jnp.float32)
    k_sh, v_sh = None, None
    for layer in range(DEPTH):
        if layer < N_A_LAYERS:
            i = layer
            h = rms_norm(x, g_mix[layer])
            x = x + rwkv7_time_mix(h, rw_mu[i], rw_w_r[i], rw_w_k[i], rw_w_v[i], rw_w_o[i],
                                   rw_w0[i], rw_w1[i], rw_w2[i], rw_a0[i], rw_a1[i], rw_a2[i],
                                   rw_g1[i], rw_g2[i], rw_k_k[i], rw_k_a[i], rw_r_k[i],
                                   rw_lnx_w[i], rw_lnx_b[i])
        else:
            j = layer - N_A_LAYERS
            if j == 0:
                k_sh, v_sh = shared_kv(x, kv_g, kv_w, k_norm_g, pos)
            lam_init = 0.8 - 0.6 * math.exp(-0.3 * layer)
            h = rms_norm(x, g_mix[layer])
            x = x + diff_attention(h, k_sh, v_sh, pos, da_w_q[j], da_q_norm_g[j],
                                   da_lam_q1[j], da_lam_k1[j], da_lam_q2[j], da_lam_k2[j],
                                   da_subln_g[j], da_w_o[j], lam_init)
        h = rms_norm(x, g_ffn[layer])
        x = x + swiglu_ffn(h, ffn_w_gu[layer], ffn_w_down[layer])
    return x
```

```python
import functools
import math

import jax
import jax.numpy as jnp
from jax import lax
from jax.experimental import pallas as pl
from jax.experimental.pallas import tpu as pltpu

F32 = jnp.float32
BF16 = jnp.bfloat16
HIGHEST = lax.Precision.HIGHEST

LANES = 128
HEAD = 64
SCAN_CHUNK = 64
ATTN_CHUNK = 64
ROPE_DIM = 16
ROPE_THETA = 500000.0
NORM_EPS = 1e-6
LNX_EPS = 64e-5
SUBLN_EPS = 1e-5
VMEM_LIMIT = 56 * 1024 * 1024


def _cparams(sem):
    return pltpu.CompilerParams(dimension_semantics=sem, vmem_limit_bytes=VMEM_LIMIT)


def _rms(x, g, eps=NORM_EPS):
    return x * lax.rsqrt(jnp.mean(x * x, axis=-1, keepdims=True) + eps) * g


def _bdot(a, b):
    return jnp.dot(a.astype(BF16), b.astype(BF16), preferred_element_type=F32)


def _group_matrix(n, group):
    r = lax.broadcasted_iota(jnp.int32, (n, n), 0) // group
    c = lax.broadcasted_iota(jnp.int32, (n, n), 1) // group
    return (r == c).astype(BF16)


def _group_sum(x, gmat):
    hi = x.astype(BF16)
    lo = (x - hi.astype(F32)).astype(BF16)
    return (jnp.dot(hi, gmat, preferred_element_type=F32)
            + jnp.dot(lo, gmat, preferred_element_type=F32))


def _rwkv_pre_kernel(seq_len, x_ref, xp_ref, g_ref, mu_ref, wr_ref, wk_ref, wv_ref,
                     w0_ref, w1_ref, w2_ref, a0_ref, a1_ref, a2_ref, g1_ref, g2_ref,
                     kk_ref, ka_ref,
                     r_out, lw_out, k_out, v_out, kkn_out, a_out, g_out):
    tm, d = x_ref.shape
    i = pl.program_id(0)
    g = g_ref[...]
    h = _rms(x_ref[...], g)
    hp = _rms(xp_ref[...], g)[7:8, :]
    hp = jnp.where((i * tm) % seq_len == 0, 0.0, hp)
    row = lax.broadcasted_iota(jnp.int32, (tm, d), 0)
    h_prev = jnp.where(row == 0, hp, pltpu.roll(h, 1, axis=0))
    hh = h_prev - h
    mu = mu_ref[...]
    xr = h + hh * mu[0:1]
    xw = h + hh * mu[1:2]
    xk = h + hh * mu[2:3]
    xv = h + hh * mu[3:4]
    xa = h + hh * mu[4:5]
    xg = h + hh * mu[5:6]

    r_out[...] = _bdot(xr, wr_ref[...])
    v_out[...] = _bdot(xv, wv_ref[...])
    k = _bdot(xk, wk_ref[...])
    wlog = w0_ref[...] + _bdot(jnp.tanh(_bdot(xw, w1_ref[...])), w2_ref[...])
    z = -wlog
    softplus = jnp.maximum(z, 0.0) + jnp.log(1.0 + jnp.exp(-jnp.abs(z)))
    lw_out[...] = -jnp.exp(-softplus - 0.5)
    a = jax.nn.sigmoid(a0_ref[...] + _bdot(_bdot(xa, a1_ref[...]), a2_ref[...]))
    a_out[...] = a
    g_out[...] = _bdot(jax.nn.sigmoid(_bdot(xg, g1_ref[...])), g2_ref[...])

    kk = k * kk_ref[...]
    gmat = _group_matrix(LANES, HEAD)
    for c in range(d // LANES):
        sl = slice(c * LANES, (c + 1) * LANES)
        kkc = kk[:, sl]
        ss = _group_sum(kkc * kkc, gmat)
        kkn_out[:, sl] = kkc / jnp.maximum(jnp.sqrt(ss), 1e-12)
    k_out[...] = k * (1.0 + (a - 1.0) * ka_ref[...])


def _pad_cols(w, n):
    return jnp.pad(w, ((0, 0), (0, n - w.shape[1])))


def _pad_rows(w, n):
    return jnp.pad(w, ((0, n - w.shape[0]), (0, 0)))


def _rwkv_pre(x2, seq_len, g_mix, mu, w_r, w_k, w_v, w0, w1, w2, a0, a1, a2, g1, g2, k_k, k_a,
              tm=512):
    t, d = x2.shape
    assert seq_len % tm == 0 and t % tm == 0
    lo_w = -(-w1.shape[1] // LANES) * LANES
    lo_a = -(-a1.shape[1] // LANES) * LANES
    lo_g = -(-g1.shape[1] // LANES) * LANES
    w1p, w2p = _pad_cols(w1, lo_w).astype(BF16), _pad_rows(w2, lo_w).astype(BF16)
    a1p, a2p = _pad_cols(a1, lo_a).astype(BF16), _pad_rows(a2, lo_a).astype(BF16)
    g1p, g2p = _pad_cols(g1, lo_g).astype(BF16), _pad_rows(g2, lo_g).astype(BF16)
    row = lambda v: v.reshape(1, d)
    const = lambda shape: pl.BlockSpec(shape, lambda i: (0, 0))
    tile = pl.BlockSpec((tm, d), lambda i: (i, 0))
    in_specs = [
        tile,
        pl.BlockSpec((8, d), lambda i: (jnp.maximum(i * (tm // 8) - 1, 0), 0)),
        const((1, d)), const((6, d)),
        const((d, d)), const((d, d)), const((d, d)),
        const((1, d)), const((d, lo_w)), const((lo_w, d)),
        const((1, d)), const((d, lo_a)), const((lo_a, d)),
        const((d, lo_g)), const((lo_g, d)),
        const((1, d)), const((1, d)),
    ]
    out_shape = [jax.ShapeDtypeStruct((t, d), F32)] * 7
    return pl.pallas_call(
        functools.partial(_rwkv_pre_kernel, seq_len),
        grid=(t // tm,),
        in_specs=in_specs,
        out_specs=[tile] * 7,
        out_shape=out_shape,
        compiler_params=_cparams(("parallel",)),
        name="rwkv_pre",
    )(x2, x2, row(g_mix), mu, w_r.astype(BF16), w_k.astype(BF16), w_v.astype(BF16),
      row(w0), w1p, w2p, row(a0), a1p, a2p, g1p, g2p, row(k_k), row(k_a))


def _rwkv_scan_kernel(r_ref, lw_ref, k_ref, v_ref, kk_ref, a_ref, rk_ref, lnw_ref, lnb_ref,
                      y_ref, z_ref):
    ts = r_ref.shape[0]
    L = SCAN_CHUNK
    n_chunks = ts // L

    @pl.when(pl.program_id(2) == 0)
    def _():
        z_ref[...] = jnp.zeros_like(z_ref)

    lane = lax.broadcasted_iota(jnp.int32, (L, LANES), 1)
    head0 = lane < HEAD
    ri = lax.broadcasted_iota(jnp.int32, (L, L), 0)
    ci = lax.broadcasted_iota(jnp.int32, (L, L), 1)
    tri_incl = (ci <= ri).astype(F32)
    r2 = lax.broadcasted_iota(jnp.int32, (2 * L, 2 * L), 0)
    c2 = lax.broadcasted_iota(jnp.int32, (2 * L, 2 * L), 1)
    same = (r2 // L) == (c2 // L)
    m_incl = same & (c2 <= r2)
    m_strict = same & (c2 < r2)
    eye = r2 == c2
    eye_f = eye.astype(F32)

    def stack(x):
        return jnp.concatenate([jnp.where(head0, x, 0.0), jnp.where(head0, 0.0, x)], axis=0)

    def hdot(a, b):
        return jnp.dot(a, b, preferred_element_type=F32, precision=HIGHEST)

    def hdot_nt(a, b):
        return lax.dot_general(a, b, (((1,), (1,)), ((), ())), preferred_element_type=F32,
                               precision=HIGHEST)

    def hdot_tn(a, b):
        return lax.dot_general(a, b, (((0,), (0,)), ((), ())), preferred_element_type=F32,
                               precision=HIGHEST)

    def chunk(c, carry):
        rows = pl.ds(pl.multiple_of(c * L, L), L)
        r = r_ref[rows, :]
        lw = lw_ref[rows, :]
        k = k_ref[rows, :]
        v = v_ref[rows, :]
        kk = kk_ref[rows, :]
        a = a_ref[rows, :]
        cum = hdot(tri_incl, lw)
        e_pos = jnp.exp(cum)
        e_neg = jnp.exp(-cum)
        wl = e_pos[L - 1:L, :]
        e_end = e_neg * wl
        b = kk * a
        At = stack(-kk * jnp.exp(cum - lw))
        Rt = stack(r * e_pos)
        Bt = stack(b * e_neg)
        Kt = stack(k * e_neg)
        Bh = stack(b * e_end)
        Kh = stack(k * e_end)
        V = stack(v)

        Aab = jnp.where(m_strict, hdot_nt(At, Bt), 0.0)
        Aak = jnp.where(m_strict, hdot_nt(At, Kt), 0.0)
        Arb = jnp.where(m_incl, hdot_nt(Rt, Bt), 0.0)
        Ark = jnp.where(m_incl, hdot_nt(Rt, Kt), 0.0)
        T = eye_f + Aab
        Apow = Aab
        for _ in range(int(math.log2(L)) - 1):
            Apow = hdot(Apow, Apow)
            T = T + hdot(T, Apow)
        P = hdot(T, At)
        Q = hdot(T, hdot(Aak, V))
        Rp = Rt + hdot(Arb, P)
        Y0 = hdot(Arb, Q) + hdot(Ark, V)
        M = jnp.where(eye, wl, 0.0) + hdot_tn(Bh, P)
        C = hdot_tn(Bh, Q) + hdot_tn(Kh, V)

        Z = z_ref[...]
        Y = hdot(Rp, Z) + Y0
        z_ref[...] = hdot(M, Z) + C
        y_ref[rows, :] = Y[0:L, :] + Y[L:2 * L, :]
        return carry

    lax.fori_loop(0, n_chunks, chunk, 0)

    gmat = _group_matrix(LANES, HEAD)
    y = y_ref[...]
    mean = _group_sum(y, gmat) * (1.0 / HEAD)
    yc = y - mean
    var = _group_sum(yc * yc, gmat) * (1.0 / HEAD)
    yn = yc * lax.rsqrt(var + LNX_EPS) * lnw_ref[...] + lnb_ref[...]
    bonus = _group_sum(r_ref[...] * k_ref[...] * rk_ref[...], gmat) * v_ref[...]
    y_ref[...] = yn + bonus


def _rwkv_scan(r, lw, k, v, kkn, a, r_k, lnx_w, lnx_b, batch, seq_len, ts=512):
    t, d = r.shape
    assert seq_len % ts == 0 and ts % SCAN_CHUNK == 0 and d % LANES == 0
    n_s = seq_len // ts
    tile = pl.BlockSpec((ts, LANES), lambda b, p, s: (b * n_s + s, p))
    vec = pl.BlockSpec((1, LANES), lambda b, p, s: (0, p))
    return pl.pallas_call(
        _rwkv_scan_kernel,
        grid=(batch, d // LANES, n_s),
        in_specs=[tile] * 6 + [vec] * 3,
        out_specs=tile,
        out_shape=jax.ShapeDtypeStruct((t, d), F32),
        scratch_shapes=[pltpu.VMEM((LANES, LANES), F32)],
        compiler_params=_cparams(("parallel", "parallel", "arbitrary")),
        name="rwkv_scan",
    )(r, lw, k, v, kkn, a, r_k.reshape(1, d), lnx_w.reshape(1, d), lnx_b.reshape(1, d))


def _proj_ffn_kernel(has_gate, *refs):
    if has_gate:
        (x_ref, a_ref, gate_ref, wo_ref, gf_ref, wg_ref, wu_ref, wd_ref,
         o_ref, x1_ref, h_ref, acc_ref) = refs
    else:
        (x_ref, a_ref, wo_ref, gf_ref, wg_ref, wu_ref, wd_ref,
         o_ref, x1_ref, h_ref, acc_ref) = refs
        gate_ref = None
    f = pl.program_id(1)

    @pl.when(f == 0)
    def _():
        a = a_ref[...]
        if has_gate:
            a = a * gate_ref[...]
        x1 = x_ref[...] + _bdot(a, wo_ref[...])
        x1_ref[...] = x1
        h_ref[...] = _rms(x1, gf_ref[...]).astype(BF16)
        acc_ref[...] = jnp.zeros_like(acc_ref)

    h = h_ref[...]
    gate = jnp.dot(h, wg_ref[...], preferred_element_type=F32)
    up = jnp.dot(h, wu_ref[...], preferred_element_type=F32)
    act = (gate * jax.nn.sigmoid(gate)) * up
    acc_ref[...] += jnp.dot(act.astype(BF16), wd_ref[...], preferred_element_type=F32)

    @pl.when(f == pl.num_programs(1) - 1)
    def _():
        o_ref[...] = x1_ref[...] + acc_ref[...]


def _proj_ffn(x2, a, gate, w_o, g_ffn, w_gu, w_down, tm=512, tf=256):
    t, d = x2.shape
    ff = w_down.shape[0]
    assert t % tm == 0 and ff % tf == 0
    n_f = ff // tf
    has_gate = gate is not None
    tile = pl.BlockSpec((tm, d), lambda i, f: (i, 0))
    in_specs = [tile, tile] + ([tile] if has_gate else []) + [
        pl.BlockSpec((d, d), lambda i, f: (0, 0)),
        pl.BlockSpec((1, d), lambda i, f: (0, 0)),
        pl.BlockSpec((d, tf), lambda i, f: (0, f)),
        pl.BlockSpec((d, tf), lambda i, f: (0, f + n_f)),
        pl.BlockSpec((tf, d), lambda i, f: (f, 0)),
    ]
    args = [x2, a] + ([gate] if has_gate else []) + [
        w_o.astype(BF16), g_ffn.reshape(1, d), w_gu.astype(BF16), w_gu.astype(BF16),
        w_down.astype(BF16)]
    return pl.pallas_call(
        functools.partial(_proj_ffn_kernel, has_gate),
        grid=(t // tm, n_f),
        in_specs=in_specs,
        out_specs=tile,
        out_shape=jax.ShapeDtypeStruct((t, d), F32),
        scratch_shapes=[pltpu.VMEM((tm, d), F32), pltpu.VMEM((tm, d), BF16),
                        pltpu.VMEM((tm, d), F32)],
        compiler_params=_cparams(("parallel", "arbitrary")),
        name="proj_ffn_gated" if has_gate else "proj_ffn",
    )(*args)


def _head_norm_rope(x, gain, cos, sin, gmat, out_ref, scale):
    d = x.shape[1]
    lane = lax.broadcasted_iota(jnp.int32, (x.shape[0], LANES), 1) % HEAD
    half = ROPE_DIM // 2
    for c in range(d // LANES):
        sl = slice(c * LANES, (c + 1) * LANES)
        xc = x[:, sl]
        ms = _group_sum(xc * xc, gmat) * (1.0 / HEAD)
        y = xc * lax.rsqrt(ms + NORM_EPS) * gain[:, sl]
        partner = jnp.where(lane < half, pltpu.roll(y, LANES - half, axis=1),
                            pltpu.roll(y, half, axis=1))
        out = y * cos + partner * sin
        if scale != 1.0:
            out = out * scale
        out_ref[:, sl] = out.astype(out_ref.dtype)


def _qkv_kernel(scale, x_ref, gkv_ref, gq_ref, wkv_ref, wq_ref, kn_ref, qn_ref, cos_ref, sin_ref,
                q_out, k_out, v_out):
    d = x_ref.shape[1]
    x = x_ref[...]
    inv = lax.rsqrt(jnp.mean(x * x, axis=-1, keepdims=True) + NORM_EPS)
    xn = x * inv
    hkv = (xn * gkv_ref[...]).astype(BF16)
    hq = (xn * gq_ref[...]).astype(BF16)
    kv = jnp.dot(hkv, wkv_ref[...], preferred_element_type=F32)
    q = jnp.dot(hq, wq_ref[...], preferred_element_type=F32)
    gmat = _group_matrix(LANES, HEAD)
    cos = cos_ref[...]
    sin = sin_ref[...]
    v_out[...] = kv[:, d:].astype(v_out.dtype)
    _head_norm_rope(kv[:, :d], kn_ref[...], cos, sin, gmat, k_out, 1.0)
    _head_norm_rope(q, qn_ref[...], cos, sin, gmat, q_out, scale)


def _rope_tables(seq_len):
    half = ROPE_DIM // 2
    pos = jnp.arange(seq_len, dtype=F32)
    inv = jnp.power(ROPE_THETA, -jnp.arange(0, ROPE_DIM, 2, dtype=F32) / ROPE_DIM)
    ang = pos[:, None] * inv[None, :]
    cos, sin = jnp.cos(ang), jnp.sin(ang)
    ones = jnp.ones((seq_len, HEAD - ROPE_DIM), F32)
    cos_h = jnp.concatenate([cos, cos, ones], axis=1)
    sin_h = jnp.concatenate([-sin, sin, 0.0 * ones], axis=1)
    reps = LANES // HEAD
    return jnp.tile(cos_h, (1, reps)), jnp.tile(sin_h, (1, reps))


def _qkv(x2, seq_len, kv_g, g_q, kv_w, w_q, k_norm_g, q_norm_g, tm=512):
    t, d = x2.shape
    assert seq_len % tm == 0
    n_s = seq_len // tm
    cos_t, sin_t = _rope_tables(seq_len)
    heads = d // HEAD
    scale = HEAD ** -0.5
    tile = pl.BlockSpec((tm, d), lambda i: (i, 0))
    const = lambda shape: pl.BlockSpec(shape, lambda i: (0, 0))
    tab = pl.BlockSpec((tm, LANES), lambda i: (i % n_s, 0))
    return pl.pallas_call(
        functools.partial(_qkv_kernel, scale),
        grid=(t // tm,),
        in_specs=[tile, const((1, d)), const((1, d)), const((d, 2 * d)), const((d, d)),
                  const((1, d)), const((1, d)), tab, tab],
        out_specs=[tile] * 3,
        out_shape=[jax.ShapeDtypeStruct((t, d), BF16)] * 3,
        compiler_params=_cparams(("parallel",)),
        name="qkv",
    )(x2, kv_g.reshape(1, d), g_q.reshape(1, d), kv_w.astype(BF16), w_q.astype(BF16),
      jnp.tile(k_norm_g, heads).reshape(1, d), jnp.tile(q_norm_g, heads).reshape(1, d),
      cos_t, sin_t)


def _diff_attn_kernel(lam_init, tk, q_ref, k_ref, v_ref, lq1_ref, lk1_ref, lq2_ref, lk2_ref,
                      sg_ref, o_ref, m_ref, l_ref, acc_ref):
    tq = q_ref.shape[0]
    qi = pl.program_id(2)
    lane = lax.broadcasted_iota(jnp.int32, (tq, LANES), 1)
    q = q_ref[...]
    zero = jnp.zeros_like(q)
    qs = (jnp.where(lane < HEAD, q, zero), jnp.where(lane < HEAD, zero, q))

    m_ref[...] = jnp.full_like(m_ref, -jnp.inf)
    l_ref[...] = jnp.zeros_like(l_ref)
    acc_ref[...] = jnp.zeros_like(acc_ref)

    def step(j, masked):
        rows = pl.ds(pl.multiple_of(j * tk, tk), tk)
        kb = k_ref[rows, :]
        vb = v_ref[rows, :]
        if masked:
            qc = (qi * tq + lax.broadcasted_iota(jnp.int32, (tq, tk), 0)) // ATTN_CHUNK
            kc = (j * tk + lax.broadcasted_iota(jnp.int32, (tq, tk), 1)) // ATTN_CHUNK
            allowed = kc <= qc
        for c in range(2):
            s = lax.dot_general(qs[c], kb, (((1,), (1,)), ((), ())),
                                preferred_element_type=F32)
            if masked:
                s = jnp.where(allowed, s, -jnp.inf)
            m_old = m_ref[c]
            m_new = jnp.maximum(m_old, jnp.max(s, axis=-1, keepdims=True))
            alpha = jnp.exp(m_old - m_new)
            p = jnp.exp(s - m_new)
            l_ref[c] = alpha * l_ref[c] + jnp.sum(p, axis=-1, keepdims=True)
            acc_ref[c] = alpha * acc_ref[c] + jnp.dot(p.astype(BF16), vb,
                                                      preferred_element_type=F32)
            m_ref[c] = m_new

    n_full = (qi * tq) // tk
    n_diag = tq // tk

    def full_body(j, carry):
        step(j, False)
        return carry

    lax.fori_loop(0, n_full, full_body, 0)
    for dj in range(n_diag):
        step(n_full + dj, True)

    lam = (jnp.exp(jnp.sum(lq1_ref[...] * lk1_ref[...], axis=-1, keepdims=True))
           - jnp.exp(jnp.sum(lq2_ref[...] * lk2_ref[...], axis=-1, keepdims=True)) + lam_init)
    o = acc_ref[0] / l_ref[0] - lam * (acc_ref[1] / l_ref[1])
    o = _rms(o, sg_ref[...], SUBLN_EPS) * (1.0 - lam_init)
    o_ref[...] = o.astype(o_ref.dtype)


def _diff_attn(q, k, v, lam_q1, lam_k1, lam_q2, lam_k2, subln_g, lam_init, batch, seq_len,
               tq=512, tk=512):
    t, d = q.shape
    assert seq_len % tq == 0 and tq % tk == 0 and tk % ATTN_CHUNK == 0
    n_q = seq_len // tq
    qtile = pl.BlockSpec((tq, LANES), lambda b, h, i: (b * n_q + i, h))
    kvfull = pl.BlockSpec((seq_len, LANES), lambda b, h, i: (b, h))
    small = lambda n: pl.BlockSpec((1, n), lambda b, h, i: (0, 0))
    return pl.pallas_call(
        functools.partial(_diff_attn_kernel, lam_init, tk),
        grid=(batch, d // LANES, n_q),
        in_specs=[qtile, kvfull, kvfull, small(HEAD), small(HEAD), small(HEAD), small(HEAD),
                  small(LANES)],
        out_specs=qtile,
        out_shape=jax.ShapeDtypeStruct((t, d), F32),
        scratch_shapes=[pltpu.VMEM((2, tq, 1), F32), pltpu.VMEM((2, tq, 1), F32),
                        pltpu.VMEM((2, tq, LANES), F32)],
        compiler_params=_cparams(("parallel", "parallel", "arbitrary")),
        name="diff_attn",
    )(q, k, v, lam_q1.reshape(1, HEAD), lam_k1.reshape(1, HEAD), lam_q2.reshape(1, HEAD),
      lam_k2.reshape(1, HEAD), subln_g.reshape(1, LANES))


def kernel(x, g_mix, g_ffn, rw_mu, rw_w_r, rw_w_k, rw_w_v, rw_w_o, rw_w0, rw_w1, rw_w2, rw_a0,
           rw_a1, rw_a2, rw_g1, rw_g2, rw_k_k, rw_k_a, rw_r_k, rw_lnx_w, rw_lnx_b, kv_g, kv_w,
           k_norm_g, da_w_q, da_q_norm_g, da_lam_q1, da_lam_k1, da_lam_q2, da_lam_k2,
           da_subln_g, da_w_o, ffn_w_gu, ffn_w_down):
    batch, seq_len, d = x.shape
    depth = g_mix.shape[0]
    n_rwkv = rw_mu.shape[0]
    x2 = x.reshape(batch * seq_len, d)
    q_src = None
    k_sh = v_sh = None
    for layer in range(depth):
        if layer < n_rwkv:
            i = layer
            r, lw, k, v, kkn, a, g = _rwkv_pre(
                x2, seq_len, g_mix[layer], rw_mu[i], rw_w_r[i], rw_w_k[i], rw_w_v[i], rw_w0[i],
                rw_w1[i], rw_w2[i], rw_a0[i], rw_a1[i], rw_a2[i], rw_g1[i], rw_g2[i],
                rw_k_k[i], rw_k_a[i])
            y = _rwkv_scan(r, lw, k, v, kkn, a, rw_r_k[i], rw_lnx_w[i], rw_lnx_b[i],
                           batch, seq_len)
            x2 = _proj_ffn(x2, y, g, rw_w_o[i], g_ffn[layer], ffn_w_gu[layer],
                           ffn_w_down[layer])
        else:
            j = layer - n_rwkv
            lam_init = 0.8 - 0.6 * math.exp(-0.3 * layer)
            if j == 0:
                q, k_sh, v_sh = _qkv(x2, seq_len, kv_g, g_mix[layer], kv_w, da_w_q[j],
                                     k_norm_g, da_q_norm_g[j])
            else:
                q, _, _ = _qkv(x2, seq_len, kv_g, g_mix[layer], kv_w, da_w_q[j],
                               k_norm_g, da_q_norm_g[j])
            o = _diff_attn(q, k_sh, v_sh, da_lam_q1[j], da_lam_k1[j], da_lam_q2[j],
                           da_lam_k2[j], da_subln_g[j], lam_init, batch, seq_len)
            x2 = _proj_ffn(x2, o, None, da_w_o[j], g_ffn[layer], ffn_w_gu[layer],
                           ffn_w_down[layer])
    return x2.reshape(batch, seq_len, d)
```

```python
import functools
import math

import jax
import jax.numpy as jnp
from jax import lax
from jax.experimental import pallas as pl
from jax.experimental.pallas import tpu as pltpu

F32 = jnp.float32
BF16 = jnp.bfloat16
HIGHEST = lax.Precision.HIGHEST

LANES = 128
HEAD = 64
SCAN_CHUNK = 64
ATTN_CHUNK = 64
ROPE_DIM = 16
ROPE_THETA = 500000.0
NORM_EPS = 1e-6
LNX_EPS = 64e-5
SUBLN_EPS = 1e-5
NEG_BIG = -1e30
VMEM_LIMIT = 56 * 1024 * 1024


def _cparams(sem):
    return pltpu.CompilerParams(dimension_semantics=sem, vmem_limit_bytes=VMEM_LIMIT)


def _rms(x, g, eps=NORM_EPS):
    return x * lax.rsqrt(jnp.mean(x * x, axis=-1, keepdims=True) + eps) * g


def _bdot(a, b):
    return jnp.dot(a.astype(BF16), b.astype(BF16), preferred_element_type=F32)


def _group_matrix(n, group):
    r = lax.broadcasted_iota(jnp.int32, (n, n), 0) // group
    c = lax.broadcasted_iota(jnp.int32, (n, n), 1) // group
    return (r == c).astype(BF16)


def _group_sum(x, gmat):
    hi = x.astype(BF16)
    lo = (x - hi.astype(F32)).astype(BF16)
    return (jnp.dot(hi, gmat, preferred_element_type=F32)
            + jnp.dot(lo, gmat, preferred_element_type=F32))


def _rwkv_pre_kernel(seq_len, x_ref, xp_ref, g_ref, mu_ref, wr_ref, wk_ref, wv_ref,
                     w0_ref, w1_ref, w2_ref, a0_ref, a1_ref, a2_ref, g1_ref, g2_ref,
                     kk_ref, ka_ref,
                     r_out, lw_out, k_out, v_out, kkn_out, a_out, g_out):
    tm, d = x_ref.shape
    i = pl.program_id(0)
    g = g_ref[...]
    h = _rms(x_ref[...], g)
    hp = _rms(xp_ref[...], g)[7:8, :]
    hp = jnp.where((i * tm) % seq_len == 0, 0.0, hp)
    row = lax.broadcasted_iota(jnp.int32, (tm, d), 0)
    h_prev = jnp.where(row == 0, hp, pltpu.roll(h, 1, axis=0))
    hh = h_prev - h
    mu = mu_ref[...]
    xr = h + hh * mu[0:1]
    xw = h + hh * mu[1:2]
    xk = h + hh * mu[2:3]
    xv = h + hh * mu[3:4]
    xa = h + hh * mu[4:5]
    xg = h + hh * mu[5:6]

    r_out[...] = _bdot(xr, wr_ref[...])
    v_out[...] = _bdot(xv, wv_ref[...])
    k = _bdot(xk, wk_ref[...])
    wlog = w0_ref[...] + _bdot(jnp.tanh(_bdot(xw, w1_ref[...])), w2_ref[...])
    z = -wlog
    softplus = jnp.maximum(z, 0.0) + jnp.log(1.0 + jnp.exp(-jnp.abs(z)))
    lw_out[...] = -jnp.exp(-softplus - 0.5)
    a = jax.nn.sigmoid(a0_ref[...] + _bdot(_bdot(xa, a1_ref[...]), a2_ref[...]))
    a_out[...] = a
    g_out[...] = _bdot(jax.nn.sigmoid(_bdot(xg, g1_ref[...])), g2_ref[...])

    kk = k * kk_ref[...]
    gmat = _group_matrix(LANES, HEAD)
    for c in range(d // LANES):
        sl = slice(c * LANES, (c + 1) * LANES)
        kkc = kk[:, sl]
        ss = _group_sum(kkc * kkc, gmat)
        kkn_out[:, sl] = kkc / jnp.maximum(jnp.sqrt(ss), 1e-12)
    k_out[...] = k * (1.0 + (a - 1.0) * ka_ref[...])


def _pad_cols(w, n):
    return jnp.pad(w, ((0, 0), (0, n - w.shape[1])))


def _pad_rows(w, n):
    return jnp.pad(w, ((0, n - w.shape[0]), (0, 0)))


def _rwkv_pre(x2, seq_len, g_mix, mu, w_r, w_k, w_v, w0, w1, w2, a0, a1, a2, g1, g2, k_k, k_a,
              tm=512):
    t, d = x2.shape
    assert seq_len % tm == 0 and t % tm == 0
    lo_w = -(-w1.shape[1] // LANES) * LANES
    lo_a = -(-a1.shape[1] // LANES) * LANES
    lo_g = -(-g1.shape[1] // LANES) * LANES
    w1p, w2p = _pad_cols(w1, lo_w).astype(BF16), _pad_rows(w2, lo_w).astype(BF16)
    a1p, a2p = _pad_cols(a1, lo_a).astype(BF16), _pad_rows(a2, lo_a).astype(BF16)
    g1p, g2p = _pad_cols(g1, lo_g).astype(BF16), _pad_rows(g2, lo_g).astype(BF16)
    row = lambda v: v.reshape(1, d)
    const = lambda shape: pl.BlockSpec(shape, lambda i: (0, 0))
    tile = pl.BlockSpec((tm, d), lambda i: (i, 0))
    in_specs = [
        tile,
        pl.BlockSpec((8, d), lambda i: (jnp.maximum(i * (tm // 8) - 1, 0), 0)),
        const((1, d)), const((6, d)),
        const((d, d)), const((d, d)), const((d, d)),
        const((1, d)), const((d, lo_w)), const((lo_w, d)),
        const((1, d)), const((d, lo_a)), const((lo_a, d)),
        const((d, lo_g)), const((lo_g, d)),
        const((1, d)), const((1, d)),
    ]
    out_shape = [jax.ShapeDtypeStruct((t, d), F32)] * 7
    return pl.pallas_call(
        functools.partial(_rwkv_pre_kernel, seq_len),
        grid=(t // tm,),
        in_specs=in_specs,
        out_specs=[tile] * 7,
        out_shape=out_shape,
        compiler_params=_cparams(("parallel",)),
        name="rwkv_pre",
    )(x2, x2, row(g_mix), mu, w_r.astype(BF16), w_k.astype(BF16), w_v.astype(BF16),
      row(w0), w1p, w2p, row(a0), a1p, a2p, g1p, g2p, row(k_k), row(k_a))


_NN = ((1,), (0,))
_NT = ((1,), (1,))
_TN = ((0,), (0,))


def _split_bf16(x):
    hi = x.astype(BF16)
    return hi, (x - hi.astype(F32)).astype(BF16)


def _mm(a, b, dims=_NN, passes=1):
    dg = lambda x, y: lax.dot_general(x, y, (dims, ((), ())), preferred_element_type=F32)
    if passes == 1:
        return dg(a.astype(BF16), b.astype(BF16))
    if passes == 3:
        ah, al = _split_bf16(a)
        bh, bl = _split_bf16(b)
        return dg(ah, bh) + (dg(ah, bl) + dg(al, bh))
    return lax.dot_general(a, b, (dims, ((), ())), preferred_element_type=F32, precision=HIGHEST)


_SCAN_PASSES = dict(a=1, t=1, s=1, z=3)


def _rwkv_scan_kernel(r_ref, lw_ref, k_ref, v_ref, kk_ref, a_ref, rk_ref, lnw_ref, lnb_ref,
                      y_ref, z_ref, rm_ref, fy_ref, gc_ref, bonus_ref):
    ts = r_ref.shape[0]
    L = SCAN_CHUNK
    L2 = 2 * L
    n_chunks = ts // L
    pa, pt, ps, pz = (_SCAN_PASSES[n] for n in "atsz")
    step = pl.program_id(2)
    cur = step % 2
    prev = 1 - cur

    @pl.when(step == 0)
    def _():
        z_ref[...] = jnp.zeros_like(z_ref)
        rm_ref[1] = jnp.zeros(rm_ref.shape[1:], F32)
        fy_ref[1] = jnp.zeros(fy_ref.shape[1:], F32)
        gc_ref[1] = jnp.zeros(gc_ref.shape[1:], F32)
        bonus_ref[1] = jnp.zeros(bonus_ref.shape[1:], F32)

    head0 = lax.broadcasted_iota(jnp.int32, (L, LANES), 1) < HEAD
    r2 = lax.broadcasted_iota(jnp.int32, (L2, 2 * L2), 0)
    c2 = lax.broadcasted_iota(jnp.int32, (L2, 2 * L2), 1) % L2
    same = (r2 // L) == (c2 // L)
    m_incl = same & (c2 <= r2)
    m_strict = same & (c2 < r2)
    eye = (lax.broadcasted_iota(jnp.int32, (L2, L2), 0)
           == lax.broadcasted_iota(jnp.int32, (L2, L2), 1))
    eye_f = eye.astype(F32)
    zeros = jnp.zeros((L2, L2), F32)

    def stack(x):
        return jnp.concatenate([jnp.where(head0, x, 0.0), jnp.where(head0, 0.0, x)], axis=0)

    r = r_ref[...]
    lw = lw_ref[...]
    k = k_ref[...]
    v = v_ref[...]
    kk = kk_ref[...]
    pos = lax.broadcasted_iota(jnp.int32, (ts, LANES), 0) % L
    cum = lw
    shift = 1
    while shift < L:
        cum = cum + jnp.where(pos >= shift, pltpu.roll(cum, shift, axis=0), 0.0)
        shift *= 2
    e_pos = jnp.exp(cum)
    e_neg = jnp.exp(-cum)
    b = kk * a_ref[...]
    at_all = -kk * jnp.exp(cum - lw)
    rt_all = r * e_pos
    bt_all = b * e_neg
    kt_all = k * e_neg

    cs = range(n_chunks)
    sls = [slice(c * L, (c + 1) * L) for c in cs]

    state = [z_ref[...]]
    pending = list(cs)

    def fold_next():
        c = pending.pop(0)
        O = _mm(rm_ref[prev, c], state[0], _NN, pz)
        Y = O[:L2] + fy_ref[prev, c]
        state[0] = O[L2:] + gc_ref[prev, c]
        y_ref[sls[c], :] = Y[:L] + Y[L:]

    wl = [e_pos[(c + 1) * L - 1:(c + 1) * L, :] for c in cs]
    At = [stack(at_all[s]) for s in sls]
    Rt = [stack(rt_all[s]) for s in sls]
    V = [stack(v[s]) for s in sls]
    Bt = [stack(bt_all[s]) for s in sls]
    Kt = [stack(kt_all[s]) for s in sls]
    BK = [jnp.concatenate([Bt[c], Kt[c]], axis=0) for c in cs]
    AA = [jnp.where(m_strict, _mm(At[c], BK[c], _NT, pa), 0.0) for c in cs]
    RR = [jnp.where(m_incl, _mm(Rt[c], BK[c], _NT, pa), 0.0) for c in cs]
    fold_next()
    AkV = [_mm(AA[c][:, L2:], V[c], _NN, ps) for c in cs]
    Apow = [AA[c][:, :L2] for c in cs]
    T = [eye_f + Apow[c] for c in cs]
    for _ in range(int(math.log2(L)) - 1):
        Apow = [_mm(Apow[c], Apow[c], _NN, pt) for c in cs]
        T = [T[c] + _mm(T[c], Apow[c], _NN, pt) for c in cs]
        fold_next()
    PQ = [_mm(T[c], jnp.concatenate([At[c], AkV[c]], axis=1), _NN, ps) for c in cs]
    rhs = [jnp.concatenate([PQ[c], jnp.concatenate([zeros, V[c]], axis=1)], axis=0) for c in cs]
    fold_next()
    F = [_mm(RR[c], rhs[c], _NN, ps) for c in cs]
    while len(pending) > 1:
        fold_next()
    G = [_mm(jnp.concatenate([Bt[c] * wl[c], Kt[c] * wl[c]], axis=0), rhs[c], _TN, ps)
         for c in cs]
    while pending:
        fold_next()
    z_ref[...] = state[0]

    gmat = _group_matrix(LANES, HEAD)
    for c in cs:
        rm_ref[cur, c] = jnp.concatenate([Rt[c] + F[c][:, :L2],
                                          jnp.where(eye, wl[c], 0.0) + G[c][:, :L2]], axis=0)
        fy_ref[cur, c] = F[c][:, L2:]
        gc_ref[cur, c] = G[c][:, L2:]
    bonus_ref[cur] = _group_sum(r * k * rk_ref[...], gmat) * v

    y = y_ref[...]
    mean = _group_sum(y, gmat) * (1.0 / HEAD)
    yc = y - mean
    var = _group_sum(yc * yc, gmat) * (1.0 / HEAD)
    yn = yc * lax.rsqrt(var + LNX_EPS) * lnw_ref[...] + lnb_ref[...]
    y_ref[...] = yn + bonus_ref[prev]


def _rwkv_scan(r, lw, k, v, kkn, a, r_k, lnx_w, lnx_b, batch, seq_len, ts=512):
    t, d = r.shape
    assert seq_len % ts == 0 and ts % SCAN_CHUNK == 0 and d % LANES == 0
    n_s = seq_len // ts
    n_chunks = ts // SCAN_CHUNK
    in_tile = pl.BlockSpec((ts, LANES), lambda b, p, s: (b * n_s + jnp.minimum(s, n_s - 1), p))
    out_tile = pl.BlockSpec((ts, LANES), lambda b, p, s: (b * n_s + jnp.maximum(s - 1, 0), p))
    vec = pl.BlockSpec((1, LANES), lambda b, p, s: (0, p))
    return pl.pallas_call(
        _rwkv_scan_kernel,
        grid=(batch, d // LANES, n_s + 1),
        in_specs=[in_tile] * 6 + [vec] * 3,
        out_specs=out_tile,
        out_shape=jax.ShapeDtypeStruct((t, d), F32),
        scratch_shapes=[pltpu.VMEM((LANES, LANES), F32),
                        pltpu.VMEM((2, n_chunks, 2 * LANES, LANES), F32),
                        pltpu.VMEM((2, n_chunks, LANES, LANES), F32),
                        pltpu.VMEM((2, n_chunks, LANES, LANES), F32),
                        pltpu.VMEM((2, ts, LANES), F32)],
        compiler_params=_cparams(("parallel", "parallel", "arbitrary")),
        name="rwkv_scan",
    )(r, lw, k, v, kkn, a, r_k.reshape(1, d), lnx_w.reshape(1, d), lnx_b.reshape(1, d))


def _proj_ffn_kernel(has_gate, *refs):
    if has_gate:
        (x_ref, a_ref, gate_ref, wo_ref, gf_ref, wg_ref, wu_ref, wd_ref,
         o_ref, x1_ref, h_ref, acc_ref) = refs
    else:
        (x_ref, a_ref, wo_ref, gf_ref, wg_ref, wu_ref, wd_ref,
         o_ref, x1_ref, h_ref, acc_ref) = refs
        gate_ref = None
    f = pl.program_id(1)

    @pl.when(f == 0)
    def _():
        a = a_ref[...]
        if has_gate:
            a = a * gate_ref[...]
        x1 = x_ref[...] + _bdot(a, wo_ref[...])
        x1_ref[...] = x1
        h_ref[...] = _rms(x1, gf_ref[...]).astype(BF16)
        acc_ref[...] = jnp.zeros_like(acc_ref)

    h = h_ref[...]
    gate = jnp.dot(h, wg_ref[...], preferred_element_type=F32)
    up = jnp.dot(h, wu_ref[...], preferred_element_type=F32)
    act = (gate * jax.nn.sigmoid(gate)) * up
    acc_ref[...] += jnp.dot(act.astype(BF16), wd_ref[...], preferred_element_type=F32)

    @pl.when(f == pl.num_programs(1) - 1)
    def _():
        o_ref[...] = x1_ref[...] + acc_ref[...]


def _proj_ffn(x2, a, gate, w_o, g_ffn, w_gu, w_down, tm=512, tf=256):
    t, d = x2.shape
    ff = w_down.shape[0]
    assert t % tm == 0 and ff % tf == 0
    n_f = ff // tf
    has_gate = gate is not None
    tile = pl.BlockSpec((tm, d), lambda i, f: (i, 0))
    in_specs = [tile, tile] + ([tile] if has_gate else []) + [
        pl.BlockSpec((d, d), lambda i, f: (0, 0)),
        pl.BlockSpec((1, d), lambda i, f: (0, 0)),
        pl.BlockSpec((d, tf), lambda i, f: (0, f)),
        pl.BlockSpec((d, tf), lambda i, f: (0, f + n_f)),
        pl.BlockSpec((tf, d), lambda i, f: (f, 0)),
    ]
    args = [x2, a] + ([gate] if has_gate else []) + [
        w_o.astype(BF16), g_ffn.reshape(1, d), w_gu.astype(BF16), w_gu.astype(BF16),
        w_down.astype(BF16)]
    return pl.pallas_call(
        functools.partial(_proj_ffn_kernel, has_gate),
        grid=(t // tm, n_f),
        in_specs=in_specs,
        out_specs=tile,
        out_shape=jax.ShapeDtypeStruct((t, d), F32),
        scratch_shapes=[pltpu.VMEM((tm, d), F32), pltpu.VMEM((tm, d), BF16),
                        pltpu.VMEM((tm, d), F32)],
        compiler_params=_cparams(("parallel", "arbitrary")),
        name="proj_ffn_gated" if has_gate else "proj_ffn",
    )(*args)


def _head_norm_rope(x, gain, cos, sin, gmat, out_ref, scale):
    d = x.shape[1]
    lane = lax.broadcasted_iota(jnp.int32, (x.shape[0], LANES), 1) % HEAD
    half = ROPE_DIM // 2
    for c in range(d // LANES):
        sl = slice(c * LANES, (c + 1) * LANES)
        xc = x[:, sl]
        ms = _group_sum(xc * xc, gmat) * (1.0 / HEAD)
        y = xc * lax.rsqrt(ms + NORM_EPS) * gain[:, sl]
        partner = jnp.where(lane < half, pltpu.roll(y, LANES - half, axis=1),
                            pltpu.roll(y, half, axis=1))
        out = y * cos + partner * sin
        if scale != 1.0:
            out = out * scale
        out_ref[:, sl] = out.astype(out_ref.dtype)


def _qkv_kernel(scale, x_ref, gkv_ref, gq_ref, wkv_ref, wq_ref, kn_ref, qn_ref, cos_ref, sin_ref,
                q_out, k_out, v_out):
    d = x_ref.shape[1]
    x = x_ref[...]
    inv = lax.rsqrt(jnp.mean(x * x, axis=-1, keepdims=True) + NORM_EPS)
    xn = x * inv
    hkv = (xn * gkv_ref[...]).astype(BF16)
    hq = (xn * gq_ref[...]).astype(BF16)
    kv = jnp.dot(hkv, wkv_ref[...], preferred_element_type=F32)
    q = jnp.dot(hq, wq_ref[...], preferred_element_type=F32)
    gmat = _group_matrix(LANES, HEAD)
    cos = cos_ref[...]
    sin = sin_ref[...]
    v_out[...] = kv[:, d:].astype(v_out.dtype)
    _head_norm_rope(kv[:, :d], kn_ref[...], cos, sin, gmat, k_out, 1.0)
    _head_norm_rope(q, qn_ref[...], cos, sin, gmat, q_out, scale)


def _rope_tables(seq_len):
    half = ROPE_DIM // 2
    pos = jnp.arange(seq_len, dtype=F32)
    inv = jnp.power(ROPE_THETA, -jnp.arange(0, ROPE_DIM, 2, dtype=F32) / ROPE_DIM)
    ang = pos[:, None] * inv[None, :]
    cos, sin = jnp.cos(ang), jnp.sin(ang)
    ones = jnp.ones((seq_len, HEAD - ROPE_DIM), F32)
    cos_h = jnp.concatenate([cos, cos, ones], axis=1)
    sin_h = jnp.concatenate([-sin, sin, 0.0 * ones], axis=1)
    reps = LANES // HEAD
    return jnp.tile(cos_h, (1, reps)), jnp.tile(sin_h, (1, reps))


def _qkv(x2, seq_len, kv_g, g_q, kv_w, w_q, k_norm_g, q_norm_g, tm=512):
    t, d = x2.shape
    assert seq_len % tm == 0
    n_s = seq_len // tm
    cos_t, sin_t = _rope_tables(seq_len)
    heads = d // HEAD
    scale = HEAD ** -0.5 * math.log2(math.e)
    tile = pl.BlockSpec((tm, d), lambda i: (i, 0))
    const = lambda shape: pl.BlockSpec(shape, lambda i: (0, 0))
    tab = pl.BlockSpec((tm, LANES), lambda i: (i % n_s, 0))
    return pl.pallas_call(
        functools.partial(_qkv_kernel, scale),
        grid=(t // tm,),
        in_specs=[tile, const((1, d)), const((1, d)), const((d, 2 * d)), const((d, d)),
                  const((1, d)), const((1, d)), tab, tab],
        out_specs=[tile] * 3,
        out_shape=[jax.ShapeDtypeStruct((t, d), BF16)] * 3,
        compiler_params=_cparams(("parallel",)),
        name="qkv",
    )(x2, kv_g.reshape(1, d), g_q.reshape(1, d), kv_w.astype(BF16), w_q.astype(BF16),
      jnp.tile(k_norm_g, heads).reshape(1, d), jnp.tile(q_norm_g, heads).reshape(1, d),
      cos_t, sin_t)


def _diff_attn_kernel(lam_init, tk, q_ref, k_ref, v_ref, lq1_ref, lk1_ref, lq2_ref, lk2_ref,
                      sg_ref, o_ref, m_ref, acc_ref):
    tq = q_ref.shape[0]
    qi = pl.program_id(2)
    lane = lax.broadcasted_iota(jnp.int32, (tq, LANES), 1)
    q = q_ref[...]
    zero = jnp.zeros_like(q)
    qs = (jnp.where(lane < HEAD, q, zero), jnp.where(lane < HEAD, zero, q))
    m_ref[...] = jnp.full_like(m_ref, NEG_BIG)
    acc_ref[...] = jnp.zeros_like(acc_ref)

    def step(start, width, masked):
        rows = pl.ds(pl.multiple_of(start, tk), width)
        kb = k_ref[rows, :]
        v_aug = jnp.concatenate([v_ref[rows, :], jnp.ones((width, LANES), BF16)], axis=1)
        s = [lax.dot_general(qs[c], kb, (_NT, ((), ())), preferred_element_type=F32)
             for c in range(2)]
        if masked:
            qc = (qi * tq + lax.broadcasted_iota(jnp.int32, (tq, width), 0)) // ATTN_CHUNK
            kc = (start + lax.broadcasted_iota(jnp.int32, (tq, width), 1)) // ATTN_CHUNK
            allowed = kc <= qc
            s = [jnp.where(allowed, sc, NEG_BIG) for sc in s]
        blocks = [[sc[:, b * LANES:(b + 1) * LANES] for b in range(width // LANES)] for sc in s]
        m_old = [m_ref[c] for c in range(2)]
        m_new = []
        for c in range(2):
            mb = blocks[c][0]
            for blk in blocks[c][1:]:
                mb = jnp.maximum(mb, blk)
            m_new.append(jnp.maximum(m_old[c], jnp.max(mb, axis=-1, keepdims=True)))
        p = [jnp.concatenate([jnp.exp2(blk - m_new[c]).astype(BF16) for blk in blocks[c]], axis=1)
             for c in range(2)]
        pv = [jnp.dot(p[c], v_aug, preferred_element_type=F32) for c in range(2)]
        for c in range(2):
            alpha = jnp.exp2(m_old[c] - m_new[c])
            acc_ref[c] = jnp.concatenate([alpha, alpha], axis=1) * acc_ref[c] + pv[c]
            m_ref[c] = m_new[c]

    n_full = (qi * tq) // tk

    def pair_body(j, carry):
        step(j * (2 * tk), 2 * tk, False)
        return carry

    lax.fori_loop(0, n_full // 2, pair_body, 0)

    @pl.when(n_full % 2 == 1)
    def _():
        step((n_full - 1) * tk, tk, False)

    step(n_full * tk, tk, True)

    lam = (jnp.exp(jnp.sum(lq1_ref[...] * lk1_ref[...], axis=-1, keepdims=True))
           - jnp.exp(jnp.sum(lq2_ref[...] * lk2_ref[...], axis=-1, keepdims=True)) + lam_init)
    a0 = acc_ref[0]
    a1 = acc_ref[1]
    o = a0[:, :LANES] / a0[:, LANES:] - lam * (a1[:, :LANES] / a1[:, LANES:])
    o = _rms(o, sg_ref[...], SUBLN_EPS) * (1.0 - lam_init)
    o_ref[...] = o.astype(o_ref.dtype)


def _diff_attn(q, k, v, lam_q1, lam_k1, lam_q2, lam_k2, subln_g, lam_init, batch, seq_len,
               tq=512, tk=512):
    t, d = q.shape
    assert seq_len % tq == 0 and tq == tk and tk % ATTN_CHUNK == 0
    n_q = seq_len // tq
    qtile = pl.BlockSpec((tq, LANES), lambda b, h, i: (b * n_q + i, h))
    kvfull = pl.BlockSpec((seq_len, LANES), lambda b, h, i: (b, h))
    small = lambda n: pl.BlockSpec((1, n), lambda b, h, i: (0, 0))
    return pl.pallas_call(
        functools.partial(_diff_attn_kernel, lam_init, tk),
        grid=(batch, d // LANES, n_q),
        in_specs=[qtile, kvfull, kvfull, small(HEAD), small(HEAD), small(HEAD), small(HEAD),
                  small(LANES)],
        out_specs=qtile,
        out_shape=jax.ShapeDtypeStruct((t, d), F32),
        scratch_shapes=[pltpu.VMEM((2, tq, LANES), F32), pltpu.VMEM((2, tq, 2 * LANES), F32)],
        compiler_params=_cparams(("parallel", "parallel", "arbitrary")),
        name="diff_attn",
    )(q, k, v, lam_q1.reshape(1, HEAD), lam_k1.reshape(1, HEAD), lam_q2.reshape(1, HEAD),
      lam_k2.reshape(1, HEAD), subln_g.reshape(1, LANES))


def kernel(x, g_mix, g_ffn, rw_mu, rw_w_r, rw_w_k, rw_w_v, rw_w_o, rw_w0, rw_w1, rw_w2, rw_a0,
           rw_a1, rw_a2, rw_g1, rw_g2, rw_k_k, rw_k_a, rw_r_k, rw_lnx_w, rw_lnx_b, kv_g, kv_w,
           k_norm_g, da_w_q, da_q_norm_g, da_lam_q1, da_lam_k1, da_lam_q2, da_lam_k2,
           da_subln_g, da_w_o, ffn_w_gu, ffn_w_down):
    batch, seq_len, d = x.shape
    depth = g_mix.shape[0]
    n_rwkv = rw_mu.shape[0]
    x2 = x.reshape(batch * seq_len, d)
    q_src = None
    k_sh = v_sh = None
    for layer in range(depth):
        if layer < n_rwkv:
            i = layer
            r, lw, k, v, kkn, a, g = _rwkv_pre(
                x2, seq_len, g_mix[layer], rw_mu[i], rw_w_r[i], rw_w_k[i], rw_w_v[i], rw_w0[i],
                rw_w1[i], rw_w2[i], rw_a0[i], rw_a1[i], rw_a2[i], rw_g1[i], rw_g2[i],
                rw_k_k[i], rw_k_a[i])
            y = _rwkv_scan(r, lw, k, v, kkn, a, rw_r_k[i], rw_lnx_w[i], rw_lnx_b[i],
                           batch, seq_len)
            x2 = _proj_ffn(x2, y, g, rw_w_o[i], g_ffn[layer], ffn_w_gu[layer],
                           ffn_w_down[layer])
        else:
            j = layer - n_rwkv
            lam_init = 0.8 - 0.6 * math.exp(-0.3 * layer)
            if j == 0:
                q, k_sh, v_sh = _qkv(x2, seq_len, kv_g, g_mix[layer], kv_w, da_w_q[j],
                                     k_norm_g, da_q_norm_g[j])
            else:
                q, _, _ = _qkv(x2, seq_len, kv_g, g_mix[layer], kv_w, da_w_q[j],
                               k_norm_g, da_q_norm_g[j])
            o = _diff_attn(q, k_sh, v_sh, da_lam_q1[j], da_lam_k1[j], da_lam_q2[j],
                           da_lam_k2[j], da_subln_g[j], lam_init, batch, seq_len)
            x2 = _proj_ffn(x2, o, None, da_w_o[j], g_ffn[layer], ffn_w_gu[layer],
                           ffn_w_down[layer])
    return x2.reshape(batch, seq_len, d)
```

```python
import functools
import math

import jax
import jax.numpy as jnp
from jax import lax
from jax.experimental import pallas as pl
from jax.experimental.pallas import tpu as pltpu

F32 = jnp.float32
BF16 = jnp.bfloat16
HIGHEST = lax.Precision.HIGHEST

LANES = 128
HEAD = 64
SCAN_CHUNK = 64
ATTN_CHUNK = 64
ROPE_DIM = 16
ROPE_THETA = 500000.0
NORM_EPS = 1e-6
LNX_EPS = 64e-5
SUBLN_EPS = 1e-5
NEG_BIG = -1e30
VMEM_LIMIT = 56 * 1024 * 1024


def _cparams(sem):
    return pltpu.CompilerParams(dimension_semantics=sem, vmem_limit_bytes=VMEM_LIMIT)


def _rms(x, g, eps=NORM_EPS):
    return x * lax.rsqrt(jnp.mean(x * x, axis=-1, keepdims=True) + eps) * g


def _bdot(a, b):
    return jnp.dot(a.astype(BF16), b.astype(BF16), preferred_element_type=F32)


def _group_matrix(n, group):
    r = lax.broadcasted_iota(jnp.int32, (n, n), 0) // group
    c = lax.broadcasted_iota(jnp.int32, (n, n), 1) // group
    return (r == c).astype(BF16)


def _group_sum(x, gmat):
    hi = x.astype(BF16)
    lo = (x - hi.astype(F32)).astype(BF16)
    return (jnp.dot(hi, gmat, preferred_element_type=F32)
            + jnp.dot(lo, gmat, preferred_element_type=F32))


def _rwkv_pre_kernel(seq_len, x_ref, xp_ref, g_ref, mu_ref, wr_ref, wk_ref, wv_ref,
                     w0_ref, w1_ref, w2_ref, a0_ref, a1_ref, a2_ref, g1_ref, g2_ref,
                     kk_ref, ka_ref,
                     r_out, lw_out, k_out, v_out, kkn_out, a_out, g_out):
    tm, d = x_ref.shape
    i = pl.program_id(0)
    g = g_ref[...]
    h = _rms(x_ref[...], g)
    hp = _rms(xp_ref[...], g)[7:8, :]
    hp = jnp.where((i * tm) % seq_len == 0, 0.0, hp)
    row = lax.broadcasted_iota(jnp.int32, (tm, d), 0)
    h_prev = jnp.where(row == 0, hp, pltpu.roll(h, 1, axis=0))
    hh = h_prev - h
    mu = mu_ref[...]
    xr = h + hh * mu[0:1]
    xw = h + hh * mu[1:2]
    xk = h + hh * mu[2:3]
    xv = h + hh * mu[3:4]
    xa = h + hh * mu[4:5]
    xg = h + hh * mu[5:6]

    r_out[...] = _bdot(xr, wr_ref[...]).astype(r_out.dtype)
    v_out[...] = _bdot(xv, wv_ref[...]).astype(v_out.dtype)
    k = _bdot(xk, wk_ref[...])
    wlog = w0_ref[...] + _bdot(jnp.tanh(_bdot(xw, w1_ref[...])), w2_ref[...])
    z = -wlog
    softplus = jnp.maximum(z, 0.0) + jnp.log(1.0 + jnp.exp(-jnp.abs(z)))
    lw_out[...] = -jnp.exp(-softplus - 0.5)
    a = jax.nn.sigmoid(a0_ref[...] + _bdot(_bdot(xa, a1_ref[...]), a2_ref[...]))
    a_out[...] = a.astype(a_out.dtype)
    g_out[...] = _bdot(jax.nn.sigmoid(_bdot(xg, g1_ref[...])), g2_ref[...]).astype(g_out.dtype)

    kk = k * kk_ref[...]
    gmat = _group_matrix(LANES, HEAD)
    for c in range(d // LANES):
        sl = slice(c * LANES, (c + 1) * LANES)
        kkc = kk[:, sl]
        ss = _group_sum(kkc * kkc, gmat)
        kkn_out[:, sl] = (kkc / jnp.maximum(jnp.sqrt(ss), 1e-12)).astype(kkn_out.dtype)
    k_out[...] = (k * (1.0 + (a - 1.0) * ka_ref[...])).astype(k_out.dtype)


def _pad_cols(w, n):
    return jnp.pad(w, ((0, 0), (0, n - w.shape[1])))


def _pad_rows(w, n):
    return jnp.pad(w, ((0, n - w.shape[0]), (0, 0)))


def _rwkv_pre(x2, seq_len, g_mix, mu, w_r, w_k, w_v, w0, w1, w2, a0, a1, a2, g1, g2, k_k, k_a,
              tm=512):
    t, d = x2.shape
    assert seq_len % tm == 0 and t % tm == 0
    lo_w = -(-w1.shape[1] // LANES) * LANES
    lo_a = -(-a1.shape[1] // LANES) * LANES
    lo_g = -(-g1.shape[1] // LANES) * LANES
    w1p, w2p = _pad_cols(w1, lo_w).astype(BF16), _pad_rows(w2, lo_w).astype(BF16)
    a1p, a2p = _pad_cols(a1, lo_a).astype(BF16), _pad_rows(a2, lo_a).astype(BF16)
    g1p, g2p = _pad_cols(g1, lo_g).astype(BF16), _pad_rows(g2, lo_g).astype(BF16)
    row = lambda v: v.reshape(1, d)
    const = lambda shape: pl.BlockSpec(shape, lambda i: (0, 0))
    tile = pl.BlockSpec((tm, d), lambda i: (i, 0))
    in_specs = [
        tile,
        pl.BlockSpec((8, d), lambda i: (jnp.maximum(i * (tm // 8) - 1, 0), 0)),
        const((1, d)), const((6, d)),
        const((d, d)), const((d, d)), const((d, d)),
        const((1, d)), const((d, lo_w)), const((lo_w, d)),
        const((1, d)), const((d, lo_a)), const((lo_a, d)),
        const((d, lo_g)), const((lo_g, d)),
        const((1, d)), const((1, d)),
    ]
    out_shape = [jax.ShapeDtypeStruct((t, d), F32 if n == 1 else BF16) for n in range(7)]
    return pl.pallas_call(
        functools.partial(_rwkv_pre_kernel, seq_len),
        grid=(t // tm,),
        in_specs=in_specs,
        out_specs=[tile] * 7,
        out_shape=out_shape,
        compiler_params=_cparams(("parallel",)),
        name="rwkv_pre",
    )(x2, x2, row(g_mix), mu, w_r.astype(BF16), w_k.astype(BF16), w_v.astype(BF16),
      row(w0), w1p, w2p, row(a0), a1p, a2p, g1p, g2p, row(k_k), row(k_a))


_NN = ((1,), (0,))
_NT = ((1,), (1,))
_TN = ((0,), (0,))


def _split_bf16(x):
    hi = x.astype(BF16)
    return hi, (x - hi.astype(F32)).astype(BF16)


def _mm(a, b, dims=_NN, passes=1):
    dg = lambda x, y: lax.dot_general(x, y, (dims, ((), ())), preferred_element_type=F32)
    if passes == 1:
        return dg(a.astype(BF16), b.astype(BF16))
    if passes == 3:
        ah, al = _split_bf16(a)
        bh, bl = _split_bf16(b)
        return dg(ah, bh) + (dg(ah, bl) + dg(al, bh))
    return lax.dot_general(a, b, (dims, ((), ())), preferred_element_type=F32, precision=HIGHEST)


_SCAN_PASSES = dict(a=1, t=1, s=1, z=1)


def _rwkv_scan_kernel(r_ref, lw_ref, k_ref, v_ref, kk_ref, a_ref, g_ref, rk_ref, lnw_ref, lnb_ref,
                      o_ref, z_ref, rm_ref, fy_ref, gc_ref, bonus_ref, gate_ref, y_ref):
    ts = r_ref.shape[0]
    L = SCAN_CHUNK
    L2 = 2 * L
    n_chunks = ts // L
    pa, pt, ps, pz = (_SCAN_PASSES[n] for n in "atsz")
    step = pl.program_id(2)
    cur = step % 2
    prev = 1 - cur

    @pl.when(step == 0)
    def _():
        z_ref[...] = jnp.zeros_like(z_ref)
        rm_ref[1] = jnp.zeros(rm_ref.shape[1:], F32)
        fy_ref[1] = jnp.zeros(fy_ref.shape[1:], F32)
        gc_ref[1] = jnp.zeros(gc_ref.shape[1:], F32)
        bonus_ref[1] = jnp.zeros(bonus_ref.shape[1:], F32)
        gate_ref[1] = jnp.zeros(gate_ref.shape[1:], F32)

    head0 = lax.broadcasted_iota(jnp.int32, (L, LANES), 1) < HEAD
    r2 = lax.broadcasted_iota(jnp.int32, (L2, 2 * L2), 0)
    c2 = lax.broadcasted_iota(jnp.int32, (L2, 2 * L2), 1) % L2
    same = (r2 // L) == (c2 // L)
    m_incl = same & (c2 <= r2)
    m_strict = same & (c2 < r2)
    eye = (lax.broadcasted_iota(jnp.int32, (L2, L2), 0)
           == lax.broadcasted_iota(jnp.int32, (L2, L2), 1))
    eye_f = eye.astype(F32)
    zeros = jnp.zeros((L2, L2), F32)

    def stack(x):
        return jnp.concatenate([jnp.where(head0, x, 0.0), jnp.where(head0, 0.0, x)], axis=0)

    cs = range(n_chunks)
    sls = [slice(c * L, (c + 1) * L) for c in cs]

    state = [z_ref[...]]
    pending = list(cs)

    def fold_next():
        c = pending.pop(0)
        O = _mm(rm_ref[prev, c], state[0], _NN, pz)
        Y = O[:L2] + fy_ref[prev, c]
        state[0] = O[L2:] + gc_ref[prev, c]
        y_ref[sls[c], :] = Y[:L] + Y[L:]

    fold_next()
    r = r_ref[...].astype(F32)
    lw = lw_ref[...]
    k = k_ref[...].astype(F32)
    v = v_ref[...].astype(F32)
    kk = kk_ref[...].astype(F32)
    pos = lax.broadcasted_iota(jnp.int32, (ts, LANES), 0) % L
    cum = lw
    shift = 1
    while shift < L:
        cum = cum + jnp.where(pos >= shift, pltpu.roll(cum, shift, axis=0), 0.0)
        shift *= 2
    e_pos = jnp.exp(cum)
    e_neg = jnp.exp(-cum)
    b = kk * a_ref[...].astype(F32)
    at_all = -kk * jnp.exp(cum - lw)
    rt_all = r * e_pos
    bt_all = b * e_neg
    kt_all = k * e_neg
    fold_next()

    wl = [e_pos[(c + 1) * L - 1:(c + 1) * L, :] for c in cs]
    At = [stack(at_all[s]) for s in sls]
    Rt = [stack(rt_all[s]) for s in sls]
    V = [stack(v[s]) for s in sls]
    Bt = [stack(bt_all[s]) for s in sls]
    Kt = [stack(kt_all[s]) for s in sls]
    BK = [jnp.concatenate([Bt[c], Kt[c]], axis=0) for c in cs]
    AA = [jnp.where(m_strict, _mm(At[c], BK[c], _NT, pa), 0.0) for c in cs]
    RR = [jnp.where(m_incl, _mm(Rt[c], BK[c], _NT, pa), 0.0) for c in cs]
    fold_next()
    AkV = [_mm(AA[c][:, L2:], V[c], _NN, ps) for c in cs]
    X = [AA[c][:, :L2] for c in cs]
    T = [eye_f + X[c] for c in cs]
    X = [_mm(X[c], X[c], _NN, pt) for c in cs]
    fold_next()
    for _ in range(int(math.log2(L)) - 2):
        XT = [_mm(X[c], jnp.concatenate([X[c], T[c]], axis=1), _NN, pt) for c in cs]
        X = [XT[c][:, :L2] for c in cs]
        T = [T[c] + XT[c][:, L2:] for c in cs]
        fold_next()
    while pending:
        fold_next()
    z_ref[...] = state[0]
    T = [T[c] + _mm(X[c], T[c], _NN, pt) for c in cs]
    PQ = [_mm(T[c], jnp.concatenate([At[c], AkV[c]], axis=1), _NN, ps) for c in cs]
    rhs = [jnp.concatenate([PQ[c], jnp.concatenate([zeros, V[c]], axis=1)], axis=0) for c in cs]

    gmat = _group_matrix(LANES, HEAD)
    y = y_ref[...]
    mean = _group_sum(y, gmat) * (1.0 / HEAD)
    yc = y - mean
    var = _group_sum(yc * yc, gmat) * (1.0 / HEAD)
    yn = yc * lax.rsqrt(var + LNX_EPS) * lnw_ref[...] + lnb_ref[...]
    o_ref[...] = ((yn + bonus_ref[prev]) * gate_ref[prev]).astype(o_ref.dtype)

    F = [_mm(RR[c], rhs[c], _NN, ps) for c in cs]
    G = [_mm(jnp.concatenate([Bt[c] * wl[c], Kt[c] * wl[c]], axis=0), rhs[c], _TN, ps)
         for c in cs]
    for c in cs:
        rm_ref[cur, c] = jnp.concatenate([Rt[c] + F[c][:, :L2],
                                          jnp.where(eye, wl[c], 0.0) + G[c][:, :L2]], axis=0)
        fy_ref[cur, c] = F[c][:, L2:]
        gc_ref[cur, c] = G[c][:, L2:]
    bonus_ref[cur] = _group_sum(r * k * rk_ref[...], gmat) * v
    gate_ref[cur] = g_ref[...].astype(F32)


def _rwkv_scan(r, lw, k, v, kkn, a, g, r_k, lnx_w, lnx_b, batch, seq_len, ts=512):
    t, d = r.shape
    assert seq_len % ts == 0 and ts % SCAN_CHUNK == 0 and d % LANES == 0
    n_s = seq_len // ts
    n_chunks = ts // SCAN_CHUNK
    in_tile = pl.BlockSpec((ts, LANES), lambda b, p, s: (b * n_s + jnp.minimum(s, n_s - 1), p))
    out_tile = pl.BlockSpec((ts, LANES), lambda b, p, s: (b * n_s + jnp.maximum(s - 1, 0), p))
    vec = pl.BlockSpec((1, LANES), lambda b, p, s: (0, p))
    return pl.pallas_call(
        _rwkv_scan_kernel,
        grid=(batch, d // LANES, n_s + 1),
        in_specs=[in_tile] * 7 + [vec] * 3,
        out_specs=out_tile,
        out_shape=jax.ShapeDtypeStruct((t, d), BF16),
        scratch_shapes=[pltpu.VMEM((LANES, LANES), F32),
                        pltpu.VMEM((2, n_chunks, 2 * LANES, LANES), F32),
                        pltpu.VMEM((2, n_chunks, LANES, LANES), F32),
                        pltpu.VMEM((2, n_chunks, LANES, LANES), F32),
                        pltpu.VMEM((2, ts, LANES), F32),
                        pltpu.VMEM((2, ts, LANES), F32),
                        pltpu.VMEM((ts, LANES), F32)],
        compiler_params=_cparams(("parallel", "parallel", "arbitrary")),
        name="rwkv_scan",
    )(r, lw, k, v, kkn, a, g, r_k.reshape(1, d), lnx_w.reshape(1, d), lnx_b.reshape(1, d))


def _proj_ffn_kernel(tf, x_ref, a_ref, wo_ref, gf_ref, wgu_ref, wd_ref, o_ref, act_ref):
    ff = wd_ref.shape[0]
    x1 = x_ref[...] + jnp.dot(a_ref[...], wo_ref[...], preferred_element_type=F32)
    h = _rms(x1, gf_ref[...]).astype(BF16)
    for f in range(ff // tf):
        gate = jnp.dot(h, wgu_ref[:, f * tf:(f + 1) * tf], preferred_element_type=F32)
        up = jnp.dot(h, wgu_ref[:, ff + f * tf:ff + (f + 1) * tf], preferred_element_type=F32)
        act_ref[:, f * tf:(f + 1) * tf] = ((gate * jax.nn.sigmoid(gate)) * up).astype(BF16)
    o_ref[...] = x1 + jnp.dot(act_ref[...], wd_ref[...], preferred_element_type=F32)


def _proj_ffn(x2, a, w_o, g_ffn, w_gu, w_down, name, tm=512, tf=256):
    t, d = x2.shape
    ff = w_down.shape[0]
    assert t % tm == 0 and ff % tf == 0
    tile = pl.BlockSpec((tm, d), lambda i: (i, 0))
    resident = lambda shape: pl.BlockSpec(shape, lambda i: (0, 0), pipeline_mode=pl.Buffered(1))
    return pl.pallas_call(
        functools.partial(_proj_ffn_kernel, tf),
        grid=(t // tm,),
        in_specs=[tile, tile, resident((d, d)), resident((1, d)), resident((d, 2 * ff)),
                  resident((ff, d))],
        out_specs=tile,
        out_shape=jax.ShapeDtypeStruct((t, d), F32),
        scratch_shapes=[pltpu.VMEM((tm, ff), BF16)],
        compiler_params=_cparams(("parallel",)),
        name=name,
    )(x2, a, w_o.astype(BF16), g_ffn.reshape(1, d), w_gu.astype(BF16), w_down.astype(BF16))


def _head_norm_rope(x, gain, cos, sin, gmat, out_ref, scale):
    d = x.shape[1]
    lane = lax.broadcasted_iota(jnp.int32, (x.shape[0], LANES), 1) % HEAD
    half = ROPE_DIM // 2
    for c in range(d // LANES):
        sl = slice(c * LANES, (c + 1) * LANES)
        xc = x[:, sl]
        ms = _group_sum(xc * xc, gmat) * (1.0 / HEAD)
        y = xc * lax.rsqrt(ms + NORM_EPS) * gain[:, sl]
        partner = jnp.where(lane < half, pltpu.roll(y, LANES - half, axis=1),
                            pltpu.roll(y, half, axis=1))
        out = y * cos + partner * sin
        if scale != 1.0:
            out = out * scale
        out_ref[:, sl] = out.astype(out_ref.dtype)


def _qkv_kernel(scale, x_ref, gkv_ref, gq_ref, wkv_ref, wq_ref, kn_ref, qn_ref, cos_ref, sin_ref,
                q_out, k_out, v_out):
    d = x_ref.shape[1]
    x = x_ref[...]
    inv = lax.rsqrt(jnp.mean(x * x, axis=-1, keepdims=True) + NORM_EPS)
    xn = x * inv
    hkv = (xn * gkv_ref[...]).astype(BF16)
    hq = (xn * gq_ref[...]).astype(BF16)
    kv = jnp.dot(hkv, wkv_ref[...], preferred_element_type=F32)
    q = jnp.dot(hq, wq_ref[...], preferred_element_type=F32)
    gmat = _group_matrix(LANES, HEAD)
    cos = cos_ref[...]
    sin = sin_ref[...]
    v_out[...] = kv[:, d:].astype(v_out.dtype)
    _head_norm_rope(kv[:, :d], kn_ref[...], cos, sin, gmat, k_out, 1.0)
    _head_norm_rope(q, qn_ref[...], cos, sin, gmat, q_out, scale)


def _rope_tables(seq_len):
    half = ROPE_DIM // 2
    pos = jnp.arange(seq_len, dtype=F32)
    inv = jnp.power(ROPE_THETA, -jnp.arange(0, ROPE_DIM, 2, dtype=F32) / ROPE_DIM)
    ang = pos[:, None] * inv[None, :]
    cos, sin = jnp.cos(ang), jnp.sin(ang)
    ones = jnp.ones((seq_len, HEAD - ROPE_DIM), F32)
    cos_h = jnp.concatenate([cos, cos, ones], axis=1)
    sin_h = jnp.concatenate([-sin, sin, 0.0 * ones], axis=1)
    reps = LANES // HEAD
    return jnp.tile(cos_h, (1, reps)), jnp.tile(sin_h, (1, reps))


def _qkv(x2, seq_len, kv_g, g_q, kv_w, w_q, k_norm_g, q_norm_g, tm=512):
    t, d = x2.shape
    assert seq_len % tm == 0
    n_s = seq_len // tm
    cos_t, sin_t = _rope_tables(seq_len)
    heads = d // HEAD
    scale = HEAD ** -0.5 * math.log2(math.e)
    tile = pl.BlockSpec((tm, d), lambda i: (i, 0))
    const = lambda shape: pl.BlockSpec(shape, lambda i: (0, 0))
    tab = pl.BlockSpec((tm, LANES), lambda i: (i % n_s, 0))
    return pl.pallas_call(
        functools.partial(_qkv_kernel, scale),
        grid=(t // tm,),
        in_specs=[tile, const((1, d)), const((1, d)), const((d, 2 * d)), const((d, d)),
                  const((1, d)), const((1, d)), tab, tab],
        out_specs=[tile] * 3,
        out_shape=[jax.ShapeDtypeStruct((t, d), BF16)] * 3,
        compiler_params=_cparams(("parallel",)),
        name="qkv",
    )(x2, kv_g.reshape(1, d), g_q.reshape(1, d), kv_w.astype(BF16), w_q.astype(BF16),
      jnp.tile(k_norm_g, heads).reshape(1, d), jnp.tile(q_norm_g, heads).reshape(1, d),
      cos_t, sin_t)


def _diff_attn_kernel(lam_init, tk, group, q_ref, k_ref, v_ref, lq1_ref, lk1_ref, lq2_ref,
                      lk2_ref, sg_ref, o_ref, m_ref, acc_ref):
    tq = q_ref.shape[0]
    qi = pl.program_id(2)
    lane = lax.broadcasted_iota(jnp.int32, (tq, LANES), 1)
    q = q_ref[...]
    zero = jnp.zeros_like(q)
    qs = (jnp.where(lane < HEAD, q, zero), jnp.where(lane < HEAD, zero, q))
    m_ref[...] = jnp.full_like(m_ref, NEG_BIG)
    acc_ref[...] = jnp.zeros_like(acc_ref)

    def step(start, width, masked):
        rows = pl.ds(pl.multiple_of(start, tk), width)
        kb = k_ref[rows, :]
        v_aug = jnp.concatenate([v_ref[rows, :], jnp.ones((width, LANES), BF16)], axis=1)
        s = [lax.dot_general(qs[c], kb, (_NT, ((), ())), preferred_element_type=F32)
             for c in range(2)]
        blocks = [[sc[:, b * LANES:(b + 1) * LANES] for b in range(width // LANES)] for sc in s]
        if masked:
            n_diag = tk // LANES
            qc = lax.broadcasted_iota(jnp.int32, (tq, LANES), 0) // ATTN_CHUNK
            kcol = lax.broadcasted_iota(jnp.int32, (tq, LANES), 1)
            for b in range(n_diag):
                allowed = (kcol + b * LANES) // ATTN_CHUNK <= qc
                for c in range(2):
                    blk = blocks[c][-n_diag + b]
                    blocks[c][-n_diag + b] = jnp.where(allowed, blk, NEG_BIG)
        m_old = [m_ref[c] for c in range(2)]
        m_new = []
        for c in range(2):
            mb = blocks[c][0]
            for blk in blocks[c][1:]:
                mb = jnp.maximum(mb, blk)
            m_new.append(jnp.maximum(m_old[c], jnp.max(mb, axis=-1, keepdims=True)))
        p = [jnp.concatenate([jnp.exp2(blk - m_new[c]).astype(BF16) for blk in blocks[c]], axis=1)
             for c in range(2)]
        pv = [jnp.dot(p[c], v_aug, preferred_element_type=F32) for c in range(2)]
        for c in range(2):
            alpha = jnp.exp2(m_old[c] - m_new[c])
            acc_ref[c] = jnp.concatenate([alpha, alpha], axis=1) * acc_ref[c] + pv[c]
            m_ref[c] = m_new[c]

    n_full = qi
    rem = n_full % group

    def group_body(j, carry):
        step(j * (group * tk), group * tk, False)
        return carry

    lax.fori_loop(0, n_full // group, group_body, 0)
    for r in range(group):
        @pl.when(rem == r)
        def _():
            step((n_full - r) * tk, (r + 1) * tk, True)

    lam = (jnp.exp(jnp.sum(lq1_ref[...] * lk1_ref[...], axis=-1, keepdims=True))
           - jnp.exp(jnp.sum(lq2_ref[...] * lk2_ref[...], axis=-1, keepdims=True)) + lam_init)
    a0 = acc_ref[0]
    a1 = acc_ref[1]
    o = a0[:, :LANES] / a0[:, LANES:] - lam * (a1[:, :LANES] / a1[:, LANES:])
    o = _rms(o, sg_ref[...], SUBLN_EPS) * (1.0 - lam_init)
    o_ref[...] = o.astype(o_ref.dtype)


def _diff_attn(q, k, v, lam_q1, lam_k1, lam_q2, lam_k2, subln_g, lam_init, batch, seq_len,
               tq=512, tk=512, group=4):
    t, d = q.shape
    assert seq_len % tq == 0 and tq == tk and tk % ATTN_CHUNK == 0
    n_q = seq_len // tq
    qtile = pl.BlockSpec((tq, LANES), lambda b, h, i: (b * n_q + i, h))
    kvfull = pl.BlockSpec((seq_len, LANES), lambda b, h, i: (b, h))
    small = lambda n: pl.BlockSpec((1, n), lambda b, h, i: (0, 0))
    return pl.pallas_call(
        functools.partial(_diff_attn_kernel, lam_init, tk, group),
        grid=(batch, d // LANES, n_q),
        in_specs=[qtile, kvfull, kvfull, small(HEAD), small(HEAD), small(HEAD), small(HEAD),
                  small(LANES)],
        out_specs=qtile,
        out_shape=jax.ShapeDtypeStruct((t, d), BF16),
        scratch_shapes=[pltpu.VMEM((2, tq, LANES), F32), pltpu.VMEM((2, tq, 2 * LANES), F32)],
        compiler_params=_cparams(("parallel", "parallel", "arbitrary")),
        name="diff_attn",
    )(q, k, v, lam_q1.reshape(1, HEAD), lam_k1.reshape(1, HEAD), lam_q2.reshape(1, HEAD),
      lam_k2.reshape(1, HEAD), subln_g.reshape(1, LANES))


def kernel(x, g_mix, g_ffn, rw_mu, rw_w_r, rw_w_k, rw_w_v, rw_w_o, rw_w0, rw_w1, rw_w2, rw_a0,
           rw_a1, rw_a2, rw_g1, rw_g2, rw_k_k, rw_k_a, rw_r_k, rw_lnx_w, rw_lnx_b, kv_g, kv_w,
           k_norm_g, da_w_q, da_q_norm_g, da_lam_q1, da_lam_k1, da_lam_q2, da_lam_k2,
           da_subln_g, da_w_o, ffn_w_gu, ffn_w_down):
    batch, seq_len, d = x.shape
    depth = g_mix.shape[0]
    n_rwkv = rw_mu.shape[0]
    x2 = x.reshape(batch * seq_len, d)
    k_sh = v_sh = None
    for layer in range(depth):
        if layer < n_rwkv:
            i = layer
            r, lw, k, v, kkn, a, g = _rwkv_pre(
                x2, seq_len, g_mix[layer], rw_mu[i], rw_w_r[i], rw_w_k[i], rw_w_v[i], rw_w0[i],
                rw_w1[i], rw_w2[i], rw_a0[i], rw_a1[i], rw_a2[i], rw_g1[i], rw_g2[i],
                rw_k_k[i], rw_k_a[i])
            yg = _rwkv_scan(r, lw, k, v, kkn, a, g, rw_r_k[i], rw_lnx_w[i], rw_lnx_b[i],
                            batch, seq_len)
            x2 = _proj_ffn(x2, yg, rw_w_o[i], g_ffn[layer], ffn_w_gu[layer],
                           ffn_w_down[layer], "proj_ffn_rwkv")
        else:
            j = layer - n_rwkv
            lam_init = 0.8 - 0.6 * math.exp(-0.3 * layer)
            q, k_new, v_new = _qkv(x2, seq_len, kv_g, g_mix[layer], kv_w, da_w_q[j],
                                   k_norm_g, da_q_norm_g[j])
            if j == 0:
                k_sh, v_sh = k_new, v_new
            o = _diff_attn(q, k_sh, v_sh, da_lam_q1[j], da_lam_k1[j], da_lam_q2[j],
                           da_lam_k2[j], da_subln_g[j], lam_init, batch, seq_len)
            x2 = _proj_ffn(x2, o, da_w_o[j], g_ffn[layer], ffn_w_gu[layer],
                           ffn_w_down[layer], "proj_ffn_attn")
    return x2.reshape(batch, seq_len, d)
```

```python
import functools
import math

import jax
import jax.numpy as jnp
from jax import lax
from jax.experimental import pallas as pl
from jax.experimental.pallas import tpu as pltpu

F32 = jnp.float32
BF16 = jnp.bfloat16
HIGHEST = lax.Precision.HIGHEST

LANES = 128
HEAD = 64
SCAN_CHUNK = 64
ATTN_CHUNK = 64
ROPE_DIM = 16
ROPE_THETA = 500000.0
NORM_EPS = 1e-6
LNX_EPS = 64e-5
SUBLN_EPS = 1e-5
NEG_BIG = -1e30
VMEM_LIMIT = 56 * 1024 * 1024


def _cparams(sem):
    return pltpu.CompilerParams(dimension_semantics=sem, vmem_limit_bytes=VMEM_LIMIT)


def _rms(x, g, eps=NORM_EPS):
    return x * lax.rsqrt(jnp.mean(x * x, axis=-1, keepdims=True) + eps) * g


def _bdot(a, b):
    return jnp.dot(a.astype(BF16), b.astype(BF16), preferred_element_type=F32)


def _group_matrix(n, group):
    r = lax.broadcasted_iota(jnp.int32, (n, n), 0) // group
    c = lax.broadcasted_iota(jnp.int32, (n, n), 1) // group
    return (r == c).astype(BF16)


def _group_sum(x, gmat):
    return jnp.dot(x.astype(BF16), gmat, preferred_element_type=F32)


def _rwkv_pre_kernel(seq_len, x_ref, xp_ref, g_ref, mu_ref, wr_ref, wk_ref, wv_ref,
                     w0_ref, w1_ref, w2_ref, a0_ref, a1_ref, a2_ref, g1_ref, g2_ref,
                     kk_ref, ka_ref,
                     r_out, lw_out, k_out, v_out, kkn_out, a_out, g_out):
    tm, d = x_ref.shape
    i = pl.program_id(0)
    g = g_ref[...]
    h = _rms(x_ref[...], g)
    hp = _rms(xp_ref[...], g)[7:8, :]
    hp = jnp.where((i * tm) % seq_len == 0, 0.0, hp)
    row = lax.broadcasted_iota(jnp.int32, (tm, d), 0)
    h_prev = jnp.where(row == 0, hp, pltpu.roll(h, 1, axis=0))
    hh = h_prev - h
    mu = mu_ref[...]
    mix = lambda n: h + hh * mu[n:n + 1]

    a = jax.nn.sigmoid(a0_ref[...] + _bdot(_bdot(mix(4), a1_ref[...]), a2_ref[...]))
    a_out[...] = a.astype(a_out.dtype)
    wlog = w0_ref[...] + _bdot(jnp.tanh(_bdot(mix(1), w1_ref[...])), w2_ref[...])
    g_lo = jax.nn.sigmoid(_bdot(mix(5), g1_ref[...]))
    k = _bdot(mix(2), wk_ref[...])
    z = -wlog
    softplus = jnp.maximum(z, 0.0) + jnp.log(1.0 + jnp.exp(-jnp.abs(z)))
    lw_out[...] = -jnp.exp(-softplus - 0.5)
    g_out[...] = _bdot(g_lo, g2_ref[...]).astype(g_out.dtype)
    r_out[...] = _bdot(mix(0), wr_ref[...]).astype(r_out.dtype)
    k_out[...] = (k * (1.0 + (a - 1.0) * ka_ref[...])).astype(k_out.dtype)

    kk = k * kk_ref[...]
    gmat = _group_matrix(LANES, HEAD)
    ss = [_group_sum(jnp.square(kk[:, c * LANES:(c + 1) * LANES]), gmat)
          for c in range(d // LANES)]
    v_out[...] = _bdot(mix(3), wv_ref[...]).astype(v_out.dtype)
    for c in range(d // LANES):
        sl = slice(c * LANES, (c + 1) * LANES)
        kkn_out[:, sl] = (kk[:, sl] / jnp.maximum(jnp.sqrt(ss[c]), 1e-12)).astype(kkn_out.dtype)


def _pad_cols(w, n):
    return jnp.pad(w, ((0, 0), (0, n - w.shape[1])))


def _pad_rows(w, n):
    return jnp.pad(w, ((0, n - w.shape[0]), (0, 0)))


def _rwkv_pre(x2, seq_len, g_mix, mu, w_r, w_k, w_v, w0, w1, w2, a0, a1, a2, g1, g2, k_k, k_a,
              tm=512):
    t, d = x2.shape
    assert seq_len % tm == 0 and t % tm == 0
    lo_w = -(-w1.shape[1] // LANES) * LANES
    lo_a = -(-a1.shape[1] // LANES) * LANES
    lo_g = -(-g1.shape[1] // LANES) * LANES
    w1p, w2p = _pad_cols(w1, lo_w).astype(BF16), _pad_rows(w2, lo_w).astype(BF16)
    a1p, a2p = _pad_cols(a1, lo_a).astype(BF16), _pad_rows(a2, lo_a).astype(BF16)
    g1p, g2p = _pad_cols(g1, lo_g).astype(BF16), _pad_rows(g2, lo_g).astype(BF16)
    row = lambda v: v.reshape(1, d)
    const = lambda shape: pl.BlockSpec(shape, lambda i: (0, 0))
    tile = pl.BlockSpec((tm, d), lambda i: (i, 0))
    in_specs = [
        tile,
        pl.BlockSpec((8, d), lambda i: (jnp.maximum(i * (tm // 8) - 1, 0), 0)),
        const((1, d)), const((6, d)),
        const((d, d)), const((d, d)), const((d, d)),
        const((1, d)), const((d, lo_w)), const((lo_w, d)),
        const((1, d)), const((d, lo_a)), const((lo_a, d)),
        const((d, lo_g)), const((lo_g, d)),
        const((1, d)), const((1, d)),
    ]
    out_shape = [jax.ShapeDtypeStruct((t, d), F32 if n == 1 else BF16) for n in range(7)]
    return pl.pallas_call(
        functools.partial(_rwkv_pre_kernel, seq_len),
        grid=(t // tm,),
        in_specs=in_specs,
        out_specs=[tile] * 7,
        out_shape=out_shape,
        compiler_params=_cparams(("parallel",)),
        name="rwkv_pre",
    )(x2, x2, row(g_mix), mu, w_r.astype(BF16), w_k.astype(BF16), w_v.astype(BF16),
      row(w0), w1p, w2p, row(a0), a1p, a2p, g1p, g2p, row(k_k), row(k_a))


_NN = ((1,), (0,))
_NT = ((1,), (1,))
_TN = ((0,), (0,))


def _split_bf16(x):
    hi = x.astype(BF16)
    return hi, (x - hi.astype(F32)).astype(BF16)


def _mm(a, b, dims=_NN, passes=1):
    dg = lambda x, y: lax.dot_general(x, y, (dims, ((), ())), preferred_element_type=F32)
    if passes == 1:
        return dg(a.astype(BF16), b.astype(BF16))
    if passes == 3:
        ah, al = _split_bf16(a)
        bh, bl = _split_bf16(b)
        return dg(ah, bh) + (dg(ah, bl) + dg(al, bh))
    return lax.dot_general(a, b, (dims, ((), ())), preferred_element_type=F32, precision=HIGHEST)


_SCAN_PASSES = dict(a=1, t=1, s=1, z=1)


def _rwkv_scan_kernel(n_s, r_ref, lw_ref, k_ref, v_ref, kk_ref, a_ref, g_ref, rk_ref, lnw_ref,
                      lnb_ref, o_ref, z_ref, rm_ref, fy_ref, gc_ref, bonus_ref, gate_ref, y_ref):
    ts = r_ref.shape[0]
    L = SCAN_CHUNK
    L2 = 2 * L
    n_chunks = ts // L
    pa, pt, ps, pz = (_SCAN_PASSES[n] for n in "atsz")
    step = pl.program_id(0)
    cur = step % 2
    prev = 1 - cur

    @pl.when(step % n_s == 1)
    def _():
        z_ref[...] = jnp.zeros_like(z_ref)

    @pl.when(step == 0)
    def _():
        z_ref[...] = jnp.zeros_like(z_ref)
        rm_ref[1] = jnp.zeros(rm_ref.shape[1:], F32)
        fy_ref[1] = jnp.zeros(fy_ref.shape[1:], F32)
        gc_ref[1] = jnp.zeros(gc_ref.shape[1:], F32)
        bonus_ref[1] = jnp.zeros(bonus_ref.shape[1:], F32)
        gate_ref[1] = jnp.zeros(gate_ref.shape[1:], F32)

    head0 = lax.broadcasted_iota(jnp.int32, (L, LANES), 1) < HEAD
    r2 = lax.broadcasted_iota(jnp.int32, (L2, 2 * L2), 0)
    c2 = lax.broadcasted_iota(jnp.int32, (L2, 2 * L2), 1) % L2
    same = (r2 // L) == (c2 // L)
    m_incl = same & (c2 <= r2)
    m_strict = same & (c2 < r2)
    eye = (lax.broadcasted_iota(jnp.int32, (L2, L2), 0)
           == lax.broadcasted_iota(jnp.int32, (L2, L2), 1))
    eye_f = eye.astype(F32)
    zeros = jnp.zeros((L2, L2), F32)

    def stack(x):
        return jnp.concatenate([jnp.where(head0, x, 0.0), jnp.where(head0, 0.0, x)], axis=0)

    cs = range(n_chunks)
    sls = [slice(c * L, (c + 1) * L) for c in cs]

    state = [z_ref[...]]
    pending = list(cs)

    def fold_next():
        c = pending.pop(0)
        O = _mm(rm_ref[prev, c], state[0], _NN, pz)
        Y = O[:L2] + fy_ref[prev, c]
        state[0] = O[L2:] + gc_ref[prev, c]
        y_ref[sls[c], :] = Y[:L] + Y[L:]

    fold_next()
    pos = lax.broadcasted_iota(jnp.int32, (L, LANES), 0)
    wl, At, Rt, V, Bt, Kt, AA, RR = ([] for _ in range(8))
    for c in cs:
        sl = sls[c]
        lw = lw_ref[sl, :]
        cum = lw
        shift = 1
        while shift < L:
            cum = cum + jnp.where(pos >= shift, pltpu.roll(cum, shift, axis=0), 0.0)
            shift *= 2
        e_pos = jnp.exp(cum)
        e_neg = jnp.exp(-cum)
        kk = kk_ref[sl, :].astype(F32)
        b = kk * a_ref[sl, :].astype(F32)
        wl.append(e_pos[L - 1:L, :])
        At.append(stack(-kk * jnp.exp(cum - lw)))
        Rt.append(stack(r_ref[sl, :].astype(F32) * e_pos))
        V.append(stack(v_ref[sl, :].astype(F32)))
        Bt.append(stack(b * e_neg))
        Kt.append(stack(k_ref[sl, :].astype(F32) * e_neg))
        BK = jnp.concatenate([Bt[c], Kt[c]], axis=0)
        AA.append(jnp.where(m_strict, _mm(At[c], BK, _NT, pa), 0.0))
        RR.append(jnp.where(m_incl, _mm(Rt[c], BK, _NT, pa), 0.0))
        if c in (n_chunks // 2 - 1, n_chunks - 1):
            fold_next()
    AkV = [_mm(AA[c][:, L2:], V[c], _NN, ps) for c in cs]
    X = [AA[c][:, :L2] for c in cs]
    T = [eye_f + X[c] for c in cs]
    X = [_mm(X[c], X[c], _NN, pt) for c in cs]
    fold_next()
    for _ in range(int(math.log2(L)) - 2):
        XT = [_mm(X[c], jnp.concatenate([X[c], T[c]], axis=1), _NN, pt) for c in cs]
        X = [XT[c][:, :L2] for c in cs]
        T = [T[c] + XT[c][:, L2:] for c in cs]
        fold_next()
    while pending:
        fold_next()
    z_ref[...] = state[0]
    T = [T[c] + _mm(X[c], T[c], _NN, pt) for c in cs]
    PQ = [_mm(T[c], jnp.concatenate([At[c], AkV[c]], axis=1), _NN, ps) for c in cs]
    rhs = [jnp.concatenate([PQ[c], jnp.concatenate([zeros, V[c]], axis=1)], axis=0) for c in cs]

    gmat = _group_matrix(LANES, HEAD)
    y = y_ref[...]
    mean = _group_sum(y, gmat) * (1.0 / HEAD)
    yc = y - mean
    var = _group_sum(yc * yc, gmat) * (1.0 / HEAD)
    yn = yc * lax.rsqrt(var + LNX_EPS) * lnw_ref[...] + lnb_ref[...]
    o_ref[...] = ((yn + bonus_ref[prev]) * gate_ref[prev]).astype(o_ref.dtype)

    F = [_mm(RR[c], rhs[c], _NN, ps) for c in cs]
    G = [_mm(jnp.concatenate([Bt[c] * wl[c], Kt[c] * wl[c]], axis=0), rhs[c], _TN, ps)
         for c in cs]
    for c in cs:
        rm_ref[cur, c] = jnp.concatenate([Rt[c] + F[c][:, :L2],
                                          jnp.where(eye, wl[c], 0.0) + G[c][:, :L2]], axis=0)
        fy_ref[cur, c] = F[c][:, L2:]
        gc_ref[cur, c] = G[c][:, L2:]
    rk = r_ref[...].astype(F32) * k_ref[...].astype(F32) * rk_ref[...]
    bonus_ref[cur] = _group_sum(rk, gmat) * v_ref[...].astype(F32)
    gate_ref[cur] = g_ref[...].astype(F32)


def _rwkv_scan(r, lw, k, v, kkn, a, g, r_k, lnx_w, lnx_b, batch, seq_len, ts=512):
    t, d = r.shape
    assert seq_len % ts == 0 and seq_len >= 2 * ts and ts % SCAN_CHUNK == 0 and d % LANES == 0
    n_s = seq_len // ts
    n_chunks = ts // SCAN_CHUNK
    n_pairs = d // LANES
    n_tiles = batch * n_pairs * n_s

    def tile_block(i):
        b, p, s = i // (n_pairs * n_s), (i // n_s) % n_pairs, i % n_s
        return b * n_s + s, p

    in_tile = pl.BlockSpec((ts, LANES), lambda i: tile_block(jnp.minimum(i, n_tiles - 1)))
    out_tile = pl.BlockSpec((ts, LANES), lambda i: tile_block(jnp.maximum(i - 1, 0)))
    vec_cur = pl.BlockSpec((1, LANES), lambda i: (0, tile_block(jnp.minimum(i, n_tiles - 1))[1]))
    vec_prev = pl.BlockSpec((1, LANES), lambda i: (0, tile_block(jnp.maximum(i - 1, 0))[1]))
    return pl.pallas_call(
        functools.partial(_rwkv_scan_kernel, n_s),
        grid=(n_tiles + 1,),
        in_specs=[in_tile] * 7 + [vec_cur, vec_prev, vec_prev],
        out_specs=out_tile,
        out_shape=jax.ShapeDtypeStruct((t, d), BF16),
        scratch_shapes=[pltpu.VMEM((LANES, LANES), F32),
                        pltpu.VMEM((2, n_chunks, 2 * LANES, LANES), F32),
                        pltpu.VMEM((2, n_chunks, LANES, LANES), F32),
                        pltpu.VMEM((2, n_chunks, LANES, LANES), F32),
                        pltpu.VMEM((2, ts, LANES), F32),
                        pltpu.VMEM((2, ts, LANES), F32),
                        pltpu.VMEM((ts, LANES), F32)],
        compiler_params=_cparams(("arbitrary",)),
        name="rwkv_scan",
    )(r, lw, k, v, kkn, a, g, r_k.reshape(1, d), lnx_w.reshape(1, d), lnx_b.reshape(1, d))


def _proj_ffn_kernel(tf, x_ref, a_ref, wo_ref, gf_ref, wgu_ref, wd_ref, o_ref, act_ref):
    ff = wd_ref.shape[0]
    x1 = x_ref[...] + jnp.dot(a_ref[...], wo_ref[...], preferred_element_type=F32)
    h = _rms(x1, gf_ref[...]).astype(BF16)
    for f in range(ff // tf):
        gate = jnp.dot(h, wgu_ref[:, f * tf:(f + 1) * tf], preferred_element_type=F32)
        up = jnp.dot(h, wgu_ref[:, ff + f * tf:ff + (f + 1) * tf], preferred_element_type=F32)
        act_ref[:, f * tf:(f + 1) * tf] = ((gate * jax.nn.sigmoid(gate)) * up).astype(BF16)
    o_ref[...] = x1 + jnp.dot(act_ref[...], wd_ref[...], preferred_element_type=F32)


def _proj_ffn(x2, a, w_o, g_ffn, w_gu, w_down, name, tm=512, tf=256):
    t, d = x2.shape
    ff = w_down.shape[0]
    assert t % tm == 0 and ff % tf == 0
    tile = pl.BlockSpec((tm, d), lambda i: (i, 0))
    resident = lambda shape: pl.BlockSpec(shape, lambda i: (0, 0), pipeline_mode=pl.Buffered(1))
    return pl.pallas_call(
        functools.partial(_proj_ffn_kernel, tf),
        grid=(t // tm,),
        in_specs=[tile, tile, resident((d, d)), resident((1, d)), resident((d, 2 * ff)),
                  resident((ff, d))],
        out_specs=tile,
        out_shape=jax.ShapeDtypeStruct((t, d), F32),
        scratch_shapes=[pltpu.VMEM((tm, ff), BF16)],
        compiler_params=_cparams(("parallel",)),
        name=name,
    )(x2, a, w_o.astype(BF16), g_ffn.reshape(1, d), w_gu.astype(BF16), w_down.astype(BF16))


def _head_sumsq(x, gmat):
    return [_group_sum(jnp.square(x[:, c * LANES:(c + 1) * LANES]), gmat)
            for c in range(x.shape[1] // LANES)]


def _head_norm_rope(x, sumsq, gain, cos, sin, out_ref, scale):
    d = x.shape[1]
    lane = lax.broadcasted_iota(jnp.int32, (x.shape[0], LANES), 1) % HEAD
    half = ROPE_DIM // 2
    for c in range(d // LANES):
        sl = slice(c * LANES, (c + 1) * LANES)
        xc = x[:, sl]
        ms = sumsq[c] * (1.0 / HEAD)
        y = xc * lax.rsqrt(ms + NORM_EPS) * gain[:, sl]
        partner = jnp.where(lane < half, pltpu.roll(y, LANES - half, axis=1),
                            pltpu.roll(y, half, axis=1))
        out = y * cos + partner * sin
        if scale != 1.0:
            out = out * scale
        out_ref[:, sl] = out.astype(out_ref.dtype)


def _qkv_kernel(scale, x_ref, gkv_ref, gq_ref, wkv_ref, wq_ref, kn_ref, qn_ref, cos_ref, sin_ref,
                q_out, k_out, v_out):
    d = x_ref.shape[1]
    x = x_ref[...]
    inv = lax.rsqrt(jnp.mean(x * x, axis=-1, keepdims=True) + NORM_EPS)
    xn = x * inv
    hkv = (xn * gkv_ref[...]).astype(BF16)
    hq = (xn * gq_ref[...]).astype(BF16)
    gmat = _group_matrix(LANES, HEAD)
    cos = cos_ref[...]
    sin = sin_ref[...]
    k = jnp.dot(hkv, wkv_ref[:, :d], preferred_element_type=F32)
    q = jnp.dot(hq, wq_ref[...], preferred_element_type=F32)
    k_sumsq = _head_sumsq(k, gmat)
    _head_norm_rope(k, k_sumsq, kn_ref[...], cos, sin, k_out, 1.0)
    q_sumsq = _head_sumsq(q, gmat)
    v_out[...] = jnp.dot(hkv, wkv_ref[:, d:], preferred_element_type=F32).astype(v_out.dtype)
    _head_norm_rope(q, q_sumsq, qn_ref[...], cos, sin, q_out, scale)


def _rope_tables(seq_len):
    half = ROPE_DIM // 2
    pos = jnp.arange(seq_len, dtype=F32)
    inv = jnp.power(ROPE_THETA, -jnp.arange(0, ROPE_DIM, 2, dtype=F32) / ROPE_DIM)
    ang = pos[:, None] * inv[None, :]
    cos, sin = jnp.cos(ang), jnp.sin(ang)
    ones = jnp.ones((seq_len, HEAD - ROPE_DIM), F32)
    cos_h = jnp.concatenate([cos, cos, ones], axis=1)
    sin_h = jnp.concatenate([-sin, sin, 0.0 * ones], axis=1)
    reps = LANES // HEAD
    return jnp.tile(cos_h, (1, reps)), jnp.tile(sin_h, (1, reps))


def _qkv(x2, seq_len, kv_g, g_q, kv_w, w_q, k_norm_g, q_norm_g, tm=512):
    t, d = x2.shape
    assert seq_len % tm == 0
    n_s = seq_len // tm
    cos_t, sin_t = _rope_tables(seq_len)
    heads = d // HEAD
    scale = HEAD ** -0.5 * math.log2(math.e)
    tile = pl.BlockSpec((tm, d), lambda i: (i, 0))
    const = lambda shape: pl.BlockSpec(shape, lambda i: (0, 0))
    tab = pl.BlockSpec((tm, LANES), lambda i: (i % n_s, 0))
    return pl.pallas_call(
        functools.partial(_qkv_kernel, scale),
        grid=(t // tm,),
        in_specs=[tile, const((1, d)), const((1, d)), const((d, 2 * d)), const((d, d)),
                  const((1, d)), const((1, d)), tab, tab],
        out_specs=[tile] * 3,
        out_shape=[jax.ShapeDtypeStruct((t, d), BF16)] * 3,
        compiler_params=_cparams(("parallel",)),
        name="qkv",
    )(x2, kv_g.reshape(1, d), g_q.reshape(1, d), kv_w.astype(BF16), w_q.astype(BF16),
      jnp.tile(k_norm_g, heads).reshape(1, d), jnp.tile(q_norm_g, heads).reshape(1, d),
      cos_t, sin_t)


def _diff_attn_kernel(lam_init, tk, group, q_ref, k_ref, v_ref, lq1_ref, lk1_ref, lq2_ref,
                      lk2_ref, sg_ref, o_ref, m_ref, acc_ref):
    tq = q_ref.shape[0]
    qi = pl.program_id(2)
    lane = lax.broadcasted_iota(jnp.int32, (tq, LANES), 1)
    q = q_ref[...]
    zero = jnp.zeros_like(q)
    qs = (jnp.where(lane < HEAD, q, zero), jnp.where(lane < HEAD, zero, q))
    m_ref[...] = jnp.full_like(m_ref, NEG_BIG)
    acc_ref[...] = jnp.zeros_like(acc_ref)

    def step(start, width, masked):
        rows = pl.ds(pl.multiple_of(start, tk), width)
        kb = k_ref[rows, :]
        v_aug = jnp.concatenate([v_ref[rows, :], jnp.ones((width, LANES), BF16)], axis=1)
        s = [lax.dot_general(qs[c], kb, (_NT, ((), ())), preferred_element_type=F32)
             for c in range(2)]
        blocks = [[sc[:, b * LANES:(b + 1) * LANES] for b in range(width // LANES)] for sc in s]
        if masked:
            n_diag = tk // LANES
            qc = lax.broadcasted_iota(jnp.int32, (tq, LANES), 0) // ATTN_CHUNK
            kcol = lax.broadcasted_iota(jnp.int32, (tq, LANES), 1)
            for b in range(n_diag):
                allowed = (kcol + b * LANES) // ATTN_CHUNK <= qc
                for c in range(2):
                    blk = blocks[c][-n_diag + b]
                    blocks[c][-n_diag + b] = jnp.where(allowed, blk, NEG_BIG)
        m_old = [m_ref[c] for c in range(2)]
        m_new = []
        for c in range(2):
            mb = blocks[c][0]
            for blk in blocks[c][1:]:
                mb = jnp.maximum(mb, blk)
            m_new.append(jnp.maximum(m_old[c], jnp.max(mb, axis=-1, keepdims=True)))
        p = [jnp.concatenate([jnp.exp2(blk - m_new[c]).astype(BF16) for blk in blocks[c]], axis=1)
             for c in range(2)]
        pv = [jnp.dot(p[c], v_aug, preferred_element_type=F32) for c in range(2)]
        for c in range(2):
            alpha = jnp.exp2(m_old[c] - m_new[c])
            acc_ref[c] = jnp.concatenate([alpha, alpha], axis=1) * acc_ref[c] + pv[c]
            m_ref[c] = m_new[c]

    n_full = qi
    rem = n_full % group

    def group_body(j, carry):
        step(j * (group * tk), group * tk, False)
        return carry

    lax.fori_loop(0, n_full // group, group_body, 0)
    for r in range(group):
        @pl.when(rem == r)
        def _():
            step((n_full - r) * tk, (r + 1) * tk, True)

    lam = (jnp.exp(jnp.sum(lq1_ref[...] * lk1_ref[...], axis=-1, keepdims=True))
           - jnp.exp(jnp.sum(lq2_ref[...] * lk2_ref[...], axis=-1, keepdims=True)) + lam_init)
    a0 = acc_ref[0]
    a1 = acc_ref[1]
    o = a0[:, :LANES] / a0[:, LANES:] - lam * (a1[:, :LANES] / a1[:, LANES:])
    o = _rms(o, sg_ref[...], SUBLN_EPS) * (1.0 - lam_init)
    o_ref[...] = o.astype(o_ref.dtype)


def _diff_attn(q, k, v, lam_q1, lam_k1, lam_q2, lam_k2, subln_g, lam_init, batch, seq_len,
               tq=512, tk=512, group=6):
    t, d = q.shape
    assert seq_len % tq == 0 and tq == tk and tk % ATTN_CHUNK == 0
    n_q = seq_len // tq
    qtile = pl.BlockSpec((tq, LANES), lambda b, h, i: (b * n_q + i, h))
    kvfull = pl.BlockSpec((seq_len, LANES), lambda b, h, i: (b, h))
    small = lambda n: pl.BlockSpec((1, n), lambda b, h, i: (0, 0))
    return pl.pallas_call(
        functools.partial(_diff_attn_kernel, lam_init, tk, group),
        grid=(batch, d // LANES, n_q),
        in_specs=[qtile, kvfull, kvfull, small(HEAD), small(HEAD), small(HEAD), small(HEAD),
                  small(LANES)],
        out_specs=qtile,
        out_shape=jax.ShapeDtypeStruct((t, d), BF16),
        scratch_shapes=[pltpu.VMEM((2, tq, LANES), F32), pltpu.VMEM((2, tq, 2 * LANES), F32)],
        compiler_params=_cparams(("parallel", "parallel", "arbitrary")),
        name="diff_attn",
    )(q, k, v, lam_q1.reshape(1, HEAD), lam_k1.reshape(1, HEAD), lam_q2.reshape(1, HEAD),
      lam_k2.reshape(1, HEAD), subln_g.reshape(1, LANES))


def kernel(x, g_mix, g_ffn, rw_mu, rw_w_r, rw_w_k, rw_w_v, rw_w_o, rw_w0, rw_w1, rw_w2, rw_a0,
           rw_a1, rw_a2, rw_g1, rw_g2, rw_k_k, rw_k_a, rw_r_k, rw_lnx_w, rw_lnx_b, kv_g, kv_w,
           k_norm_g, da_w_q, da_q_norm_g, da_lam_q1, da_lam_k1, da_lam_q2, da_lam_k2,
           da_subln_g, da_w_o, ffn_w_gu, ffn_w_down):
    batch, seq_len, d = x.shape
    depth = g_mix.shape[0]
    n_rwkv = rw_mu.shape[0]
    x2 = x.reshape(batch * seq_len, d)
    k_sh = v_sh = None
    for layer in range(depth):
        if layer < n_rwkv:
            i = layer
            r, lw, k, v, kkn, a, g = _rwkv_pre(
                x2, seq_len, g_mix[layer], rw_mu[i], rw_w_r[i], rw_w_k[i], rw_w_v[i], rw_w0[i],
                rw_w1[i], rw_w2[i], rw_a0[i], rw_a1[i], rw_a2[i], rw_g1[i], rw_g2[i],
                rw_k_k[i], rw_k_a[i])
            yg = _rwkv_scan(r, lw, k, v, kkn, a, g, rw_r_k[i], rw_lnx_w[i], rw_lnx_b[i],
                            batch, seq_len)
            x2 = _proj_ffn(x2, yg, rw_w_o[i], g_ffn[layer], ffn_w_gu[layer],
                           ffn_w_down[layer], "proj_ffn_rwkv")
        else:
            j = layer - n_rwkv
            lam_init = 0.8 - 0.6 * math.exp(-0.3 * layer)
            q, k_new, v_new = _qkv(x2, seq_len, kv_g, g_mix[layer], kv_w, da_w_q[j],
                                   k_norm_g, da_q_norm_g[j])
            if j == 0:
                k_sh, v_sh = k_new, v_new
            o = _diff_attn(q, k_sh, v_sh, da_lam_q1[j], da_lam_k1[j], da_lam_q2[j],
                           da_lam_k2[j], da_subln_g[j], lam_init, batch, seq_len)
            x2 = _proj_ffn(x2, o, da_w_o[j], g_ffn[layer], ffn_w_gu[layer],
                           ffn_w_down[layer], "proj_ffn_attn")
    return x2.reshape(batch, seq_len, d)
```

```python
import functools
import math

import jax
import jax.numpy as jnp
from jax import lax
from jax.experimental import pallas as pl
from jax.experimental.pallas import tpu as pltpu

F32 = jnp.float32
BF16 = jnp.bfloat16
HIGHEST = lax.Precision.HIGHEST

LANES = 128
HEAD = 64
SCAN_CHUNK = 64
ATTN_CHUNK = 64
ROPE_DIM = 16
ROPE_THETA = 500000.0
NORM_EPS = 1e-6
LNX_EPS = 64e-5
SUBLN_EPS = 1e-5
NEG_BIG = -1e30
VMEM_LIMIT = 56 * 1024 * 1024


def _cparams(sem):
    return pltpu.CompilerParams(dimension_semantics=sem, vmem_limit_bytes=VMEM_LIMIT)


def _rms(x, g, eps=NORM_EPS):
    return x * lax.rsqrt(jnp.mean(x * x, axis=-1, keepdims=True) + eps) * g


def _bdot(a, b):
    return jnp.dot(a.astype(BF16), b.astype(BF16), preferred_element_type=F32)


def _group_matrix(n, group):
    r = lax.broadcasted_iota(jnp.int32, (n, n), 0) // group
    c = lax.broadcasted_iota(jnp.int32, (n, n), 1) // group
    return (r == c).astype(BF16)


def _group_sum(x, gmat):
    return jnp.dot(x.astype(BF16), gmat, preferred_element_type=F32)


def _rwkv_pre_kernel(seq_len, x_ref, xp_ref, g_ref, mu_ref, wr_ref, wk_ref, wv_ref,
                     w0_ref, w1_ref, w2_ref, a0_ref, a1_ref, a2_ref, g1_ref, g2_ref,
                     kk_ref, ka_ref,
                     r_out, lw_out, k_out, v_out, kkn_out, a_out, g_out):
    tm, d = x_ref.shape
    i = pl.program_id(0)
    g = g_ref[...]
    h = _rms(x_ref[...], g)
    hp = _rms(xp_ref[...], g)[7:8, :]
    hp = jnp.where((i * tm) % seq_len == 0, 0.0, hp)
    row = lax.broadcasted_iota(jnp.int32, (tm, d), 0)
    h_prev = jnp.where(row == 0, hp, pltpu.roll(h, 1, axis=0))
    hh = h_prev - h
    mu = mu_ref[...]
    mix = lambda n: h + hh * mu[n:n + 1]

    r_out[...] = _bdot(mix(0), wr_ref[...]).astype(r_out.dtype)
    v_out[...] = _bdot(mix(3), wv_ref[...]).astype(v_out.dtype)
    k = _bdot(mix(2), wk_ref[...])
    wlog = w0_ref[...] + _bdot(jnp.tanh(_bdot(mix(1), w1_ref[...])), w2_ref[...])
    lw_out[...] = -math.exp(-0.5) * jax.nn.sigmoid(wlog)
    a = jax.nn.sigmoid(a0_ref[...] + _bdot(_bdot(mix(4), a1_ref[...]), a2_ref[...]))
    a_out[...] = a.astype(a_out.dtype)
    g_out[...] = _bdot(jax.nn.sigmoid(_bdot(mix(5), g1_ref[...])), g2_ref[...]).astype(g_out.dtype)

    kk = k * kk_ref[...]
    gmat = _group_matrix(LANES, HEAD)
    for c in range(d // LANES):
        sl = slice(c * LANES, (c + 1) * LANES)
        ss = _group_sum(jnp.square(kk[:, sl]), gmat)
        kkn_out[:, sl] = (kk[:, sl] * lax.rsqrt(jnp.maximum(ss, 1e-24))).astype(kkn_out.dtype)
    k_out[...] = (k * (1.0 + (a - 1.0) * ka_ref[...])).astype(k_out.dtype)


def _pad_cols(w, n):
    return jnp.pad(w, ((0, 0), (0, n - w.shape[1])))


def _pad_rows(w, n):
    return jnp.pad(w, ((0, n - w.shape[0]), (0, 0)))


def _rwkv_pre(x2, seq_len, g_mix, mu, w_r, w_k, w_v, w0, w1, w2, a0, a1, a2, g1, g2, k_k, k_a,
              tm=512):
    t, d = x2.shape
    assert seq_len % tm == 0 and t % tm == 0
    lo_w = -(-w1.shape[1] // LANES) * LANES
    lo_a = -(-a1.shape[1] // LANES) * LANES
    lo_g = -(-g1.shape[1] // LANES) * LANES
    w1p, w2p = _pad_cols(w1, lo_w).astype(BF16), _pad_rows(w2, lo_w).astype(BF16)
    a1p, a2p = _pad_cols(a1, lo_a).astype(BF16), _pad_rows(a2, lo_a).astype(BF16)
    g1p, g2p = _pad_cols(g1, lo_g).astype(BF16), _pad_rows(g2, lo_g).astype(BF16)
    row = lambda v: v.reshape(1, d)
    const = lambda shape: pl.BlockSpec(shape, lambda i: (0, 0))
    tile = pl.BlockSpec((tm, d), lambda i: (i, 0))
    in_specs = [
        tile,
        pl.BlockSpec((8, d), lambda i: (jnp.maximum(i * (tm // 8) - 1, 0), 0)),
        const((1, d)), const((6, d)),
        const((d, d)), const((d, d)), const((d, d)),
        const((1, d)), const((d, lo_w)), const((lo_w, d)),
        const((1, d)), const((d, lo_a)), const((lo_a, d)),
        const((d, lo_g)), const((lo_g, d)),
        const((1, d)), const((1, d)),
    ]
    out_shape = [jax.ShapeDtypeStruct((t, d), F32 if n == 1 else BF16) for n in range(7)]
    return pl.pallas_call(
        functools.partial(_rwkv_pre_kernel, seq_len),
        grid=(t // tm,),
        in_specs=in_specs,
        out_specs=[tile] * 7,
        out_shape=out_shape,
        compiler_params=_cparams(("parallel",)),
        name="rwkv_pre",
    )(x2, x2, row(g_mix), mu, w_r.astype(BF16), w_k.astype(BF16), w_v.astype(BF16),
      row(w0), w1p, w2p, row(a0), a1p, a2p, g1p, g2p, row(k_k), row(k_a))


_NN = ((1,), (0,))
_NT = ((1,), (1,))
_TN = ((0,), (0,))


def _split_bf16(x):
    hi = x.astype(BF16)
    return hi, (x - hi.astype(F32)).astype(BF16)


def _mm(a, b, dims=_NN, passes=1):
    dg = lambda x, y: lax.dot_general(x, y, (dims, ((), ())), preferred_element_type=F32)
    if passes == 1:
        return dg(a.astype(BF16), b.astype(BF16))
    if passes == 3:
        ah, al = _split_bf16(a)
        bh, bl = _split_bf16(b)
        return dg(ah, bh) + (dg(ah, bl) + dg(al, bh))
    return lax.dot_general(a, b, (dims, ((), ())), preferred_element_type=F32, precision=HIGHEST)


_SCAN_PASSES = dict(a=1, t=1, s=1, z=1)


def _rwkv_scan_kernel(n_s, r_ref, lw_ref, k_ref, v_ref, kk_ref, a_ref, g_ref, rk_ref, lnw_ref,
                      lnb_ref, o_ref, z_ref, rm_ref, fy_ref, gc_ref, bonus_ref, gate_ref, y_ref):
    ts = r_ref.shape[0]
    L = SCAN_CHUNK
    L2 = 2 * L
    n_chunks = ts // L
    pa, pt, ps, pz = (_SCAN_PASSES[n] for n in "atsz")
    step = pl.program_id(0)
    cur = step % 2
    prev = 1 - cur

    @pl.when(step % n_s == 1)
    def _():
        z_ref[...] = jnp.zeros_like(z_ref)

    @pl.when(step == 0)
    def _():
        z_ref[...] = jnp.zeros_like(z_ref)
        rm_ref[1] = jnp.zeros(rm_ref.shape[1:], F32)
        fy_ref[1] = jnp.zeros(fy_ref.shape[1:], F32)
        gc_ref[1] = jnp.zeros(gc_ref.shape[1:], F32)
        bonus_ref[1] = jnp.zeros(bonus_ref.shape[1:], F32)
        gate_ref[1] = jnp.zeros(gate_ref.shape[1:], F32)

    head0 = lax.broadcasted_iota(jnp.int32, (L, LANES), 1) < HEAD
    r2 = lax.broadcasted_iota(jnp.int32, (L2, 2 * L2), 0)
    c2 = lax.broadcasted_iota(jnp.int32, (L2, 2 * L2), 1) % L2
    same = (r2 // L) == (c2 // L)
    m_incl = same & (c2 <= r2)
    m_strict = same & (c2 < r2)
    eye = (lax.broadcasted_iota(jnp.int32, (L2, L2), 0)
           == lax.broadcasted_iota(jnp.int32, (L2, L2), 1))
    eye_f = eye.astype(F32)
    zeros = jnp.zeros((L2, L2), F32)

    def stack(x):
        return jnp.concatenate([jnp.where(head0, x, 0.0), jnp.where(head0, 0.0, x)], axis=0)

    cs = range(n_chunks)
    sls = [slice(c * L, (c + 1) * L) for c in cs]

    state = [z_ref[...]]
    pending = list(cs)

    def fold_next():
        c = pending.pop(0)
        O = _mm(rm_ref[prev, c], state[0], _NN, pz)
        Y = O[:L2] + fy_ref[prev, c]
        state[0] = O[L2:] + gc_ref[prev, c]
        y_ref[sls[c], :] = Y[:L] + Y[L:]

    fold_next()
    pos = lax.broadcasted_iota(jnp.int32, (L, LANES), 0)
    wl, At, Rt, V, Bt, Kt, AA, RR = ([] for _ in range(8))
    for c in cs:
        sl = sls[c]
        lw = lw_ref[sl, :]
        cum = lw
        shift = 1
        while shift < L:
            cum = cum + jnp.where(pos >= shift, pltpu.roll(cum, shift, axis=0), 0.0)
            shift *= 2
        e_pos = jnp.exp(cum)
        e_neg = jnp.exp(-cum)
        kk = kk_ref[sl, :].astype(F32)
        b = kk * a_ref[sl, :].astype(F32)
        wl.append(e_pos[L - 1:L, :])
        At.append(stack(-kk * jnp.exp(cum - lw)))
        Rt.append(stack(r_ref[sl, :].astype(F32) * e_pos))
        V.append(stack(v_ref[sl, :].astype(F32)))
        Bt.append(stack(b * e_neg))
        Kt.append(stack(k_ref[sl, :].astype(F32) * e_neg))
        BK = jnp.concatenate([Bt[c], Kt[c]], axis=0)
        AA.append(jnp.where(m_strict, _mm(At[c], BK, _NT, pa), 0.0))
        RR.append(jnp.where(m_incl, _mm(Rt[c], BK, _NT, pa), 0.0))
        if c in (n_chunks // 2 - 1, n_chunks - 1):
            fold_next()
    AkV = [_mm(AA[c][:, L2:], V[c], _NN, ps) for c in cs]
    X = [AA[c][:, :L2] for c in cs]
    T = [eye_f + X[c] for c in cs]
    X = [_mm(X[c], X[c], _NN, pt) for c in cs]
    fold_next()
    for _ in range(int(math.log2(L)) - 2):
        XT = [_mm(X[c], jnp.concatenate([X[c], T[c]], axis=1), _NN, pt) for c in cs]
        X = [XT[c][:, :L2] for c in cs]
        T = [T[c] + XT[c][:, L2:] for c in cs]
        fold_next()
    while pending:
        fold_next()
    z_ref[...] = state[0]
    T = [T[c] + _mm(X[c], T[c], _NN, pt) for c in cs]
    PQ = [_mm(T[c], jnp.concatenate([At[c], AkV[c]], axis=1), _NN, ps) for c in cs]
    rhs = [jnp.concatenate([PQ[c], jnp.concatenate([zeros, V[c]], axis=1)], axis=0) for c in cs]

    gmat = _group_matrix(LANES, HEAD)
    y = y_ref[...]
    mean = _group_sum(y, gmat) * (1.0 / HEAD)
    yc = y - mean
    var = _group_sum(yc * yc, gmat) * (1.0 / HEAD)
    yn = yc * lax.rsqrt(var + LNX_EPS) * lnw_ref[...] + lnb_ref[...]
    o_ref[...] = ((yn + bonus_ref[prev]) * gate_ref[prev]).astype(o_ref.dtype)

    F = [_mm(RR[c], rhs[c], _NN, ps) for c in cs]
    G = [_mm(jnp.concatenate([Bt[c] * wl[c], Kt[c] * wl[c]], axis=0), rhs[c], _TN, ps)
         for c in cs]
    for c in cs:
        rm_ref[cur, c] = jnp.concatenate([Rt[c] + F[c][:, :L2],
                                          jnp.where(eye, wl[c], 0.0) + G[c][:, :L2]], axis=0)
        fy_ref[cur, c] = F[c][:, L2:]
        gc_ref[cur, c] = G[c][:, L2:]
    rk = r_ref[...].astype(F32) * k_ref[...].astype(F32) * rk_ref[...]
    bonus_ref[cur] = _group_sum(rk, gmat) * v_ref[...].astype(F32)
    gate_ref[cur] = g_ref[...].astype(F32)


def _rwkv_scan(r, lw, k, v, kkn, a, g, r_k, lnx_w, lnx_b, batch, seq_len, ts=512):
    t, d = r.shape
    assert seq_len % ts == 0 and seq_len >= 2 * ts and ts % SCAN_CHUNK == 0 and d % LANES == 0
    n_s = seq_len // ts
    n_chunks = ts // SCAN_CHUNK
    n_pairs = d // LANES
    n_tiles = batch * n_pairs * n_s

    def tile_block(i):
        b, p, s = i // (n_pairs * n_s), (i // n_s) % n_pairs, i % n_s
        return b * n_s + s, p

    in_tile = pl.BlockSpec((ts, LANES), lambda i: tile_block(jnp.minimum(i, n_tiles - 1)))
    out_tile = pl.BlockSpec((ts, LANES), lambda i: tile_block(jnp.maximum(i - 1, 0)))
    vec_cur = pl.BlockSpec((1, LANES), lambda i: (0, tile_block(jnp.minimum(i, n_tiles - 1))[1]))
    vec_prev = pl.BlockSpec((1, LANES), lambda i: (0, tile_block(jnp.maximum(i - 1, 0))[1]))
    return pl.pallas_call(
        functools.partial(_rwkv_scan_kernel, n_s),
        grid=(n_tiles + 1,),
        in_specs=[in_tile] * 7 + [vec_cur, vec_prev, vec_prev],
        out_specs=out_tile,
        out_shape=jax.ShapeDtypeStruct((t, d), BF16),
        scratch_shapes=[pltpu.VMEM((LANES, LANES), F32),
                        pltpu.VMEM((2, n_chunks, 2 * LANES, LANES), F32),
                        pltpu.VMEM((2, n_chunks, LANES, LANES), F32),
                        pltpu.VMEM((2, n_chunks, LANES, LANES), F32),
                        pltpu.VMEM((2, ts, LANES), F32),
                        pltpu.VMEM((2, ts, LANES), F32),
                        pltpu.VMEM((ts, LANES), F32)],
        compiler_params=_cparams(("arbitrary",)),
        name="rwkv_scan",
    )(r, lw, k, v, kkn, a, g, r_k.reshape(1, d), lnx_w.reshape(1, d), lnx_b.reshape(1, d))


def _proj_ffn_kernel(tf, x_ref, a_ref, wo_ref, gf_ref, wgu_ref, wd_ref, o_ref, act_ref):
    ff = wd_ref.shape[0]
    x1 = x_ref[...] + jnp.dot(a_ref[...], wo_ref[...], preferred_element_type=F32)
    h = _rms(x1, gf_ref[...]).astype(BF16)
    for f in range(ff // tf):
        gate = jnp.dot(h, wgu_ref[:, f * tf:(f + 1) * tf], preferred_element_type=F32)
        up = jnp.dot(h, wgu_ref[:, ff + f * tf:ff + (f + 1) * tf], preferred_element_type=F32)
        act_ref[:, f * tf:(f + 1) * tf] = ((gate * jax.nn.sigmoid(gate)) * up).astype(BF16)
    o_ref[...] = x1 + jnp.dot(act_ref[...], wd_ref[...], preferred_element_type=F32)


def _proj_ffn(x2, a, w_o, g_ffn, w_gu, w_down, name, tm=512, tf=256):
    t, d = x2.shape
    ff = w_down.shape[0]
    assert t % tm == 0 and ff % tf == 0
    tile = pl.BlockSpec((tm, d), lambda i: (i, 0))
    resident = lambda shape: pl.BlockSpec(shape, lambda i: (0, 0), pipeline_mode=pl.Buffered(1))
    return pl.pallas_call(
        functools.partial(_proj_ffn_kernel, tf),
        grid=(t // tm,),
        in_specs=[tile, tile, resident((d, d)), resident((1, d)), resident((d, 2 * ff)),
                  resident((ff, d))],
        out_specs=tile,
        out_shape=jax.ShapeDtypeStruct((t, d), F32),
        scratch_shapes=[pltpu.VMEM((tm, ff), BF16)],
        compiler_params=_cparams(("parallel",)),
        name=name,
    )(x2, a, w_o.astype(BF16), g_ffn.reshape(1, d), w_gu.astype(BF16), w_down.astype(BF16))


def _head_sumsq(x, gmat):
    return [_group_sum(jnp.square(x[:, c * LANES:(c + 1) * LANES]), gmat)
            for c in range(x.shape[1] // LANES)]


def _head_norm_rope(x, sumsq, gain, cos, sin, out_ref, scale):
    d = x.shape[1]
    lane = lax.broadcasted_iota(jnp.int32, (x.shape[0], LANES), 1) % HEAD
    half = ROPE_DIM // 2
    for c in range(d // LANES):
        sl = slice(c * LANES, (c + 1) * LANES)
        xc = x[:, sl]
        ms = sumsq[c] * (1.0 / HEAD)
        y = xc * lax.rsqrt(ms + NORM_EPS) * gain[:, sl]
        partner = jnp.where(lane < half, pltpu.roll(y, LANES - half, axis=1),
                            pltpu.roll(y, half, axis=1))
        out = y * cos + partner * sin
        if scale != 1.0:
            out = out * scale
        out_ref[:, sl] = out.astype(out_ref.dtype)


def _qkv_kernel(scale, x_ref, gkv_ref, gq_ref, wkv_ref, wq_ref, kn_ref, qn_ref, cos_ref, sin_ref,
                q_out, k_out, v_out):
    d = x_ref.shape[1]
    x = x_ref[...]
    inv = lax.rsqrt(jnp.mean(x * x, axis=-1, keepdims=True) + NORM_EPS)
    xn = x * inv
    hkv = (xn * gkv_ref[...]).astype(BF16)
    hq = (xn * gq_ref[...]).astype(BF16)
    gmat = _group_matrix(LANES, HEAD)
    cos = cos_ref[...]
    sin = sin_ref[...]
    k = jnp.dot(hkv, wkv_ref[:, :d], preferred_element_type=F32)
    q = jnp.dot(hq, wq_ref[...], preferred_element_type=F32)
    k_sumsq = _head_sumsq(k, gmat)
    _head_norm_rope(k, k_sumsq, kn_ref[...], cos, sin, k_out, 1.0)
    q_sumsq = _head_sumsq(q, gmat)
    v_out[...] = jnp.dot(hkv, wkv_ref[:, d:], preferred_element_type=F32).astype(v_out.dtype)
    _head_norm_rope(q, q_sumsq, qn_ref[...], cos, sin, q_out, scale)


def _rope_tables(seq_len):
    half = ROPE_DIM // 2
    pos = jnp.arange(seq_len, dtype=F32)
    inv = jnp.power(ROPE_THETA, -jnp.arange(0, ROPE_DIM, 2, dtype=F32) / ROPE_DIM)
    ang = pos[:, None] * inv[None, :]
    cos, sin = jnp.cos(ang), jnp.sin(ang)
    ones = jnp.ones((seq_len, HEAD - ROPE_DIM), F32)
    cos_h = jnp.concatenate([cos, cos, ones], axis=1)
    sin_h = jnp.concatenate([-sin, sin, 0.0 * ones], axis=1)
    reps = LANES // HEAD
    return jnp.tile(cos_h, (1, reps)), jnp.tile(sin_h, (1, reps))


def _qkv(x2, seq_len, kv_g, g_q, kv_w, w_q, k_norm_g, q_norm_g, tm=512):
    t, d = x2.shape
    assert seq_len % tm == 0
    n_s = seq_len // tm
    cos_t, sin_t = _rope_tables(seq_len)
    heads = d // HEAD
    scale = HEAD ** -0.5 * math.log2(math.e)
    tile = pl.BlockSpec((tm, d), lambda i: (i, 0))
    const = lambda shape: pl.BlockSpec(shape, lambda i: (0, 0))
    tab = pl.BlockSpec((tm, LANES), lambda i: (i % n_s, 0))
    return pl.pallas_call(
        functools.partial(_qkv_kernel, scale),
        grid=(t // tm,),
        in_specs=[tile, const((1, d)), const((1, d)), const((d, 2 * d)), const((d, d)),
                  const((1, d)), const((1, d)), tab, tab],
        out_specs=[tile] * 3,
        out_shape=[jax.ShapeDtypeStruct((t, d), BF16)] * 3,
        compiler_params=_cparams(("parallel",)),
        name="qkv",
    )(x2, kv_g.reshape(1, d), g_q.reshape(1, d), kv_w.astype(BF16), w_q.astype(BF16),
      jnp.tile(k_norm_g, heads).reshape(1, d), jnp.tile(q_norm_g, heads).reshape(1, d),
      cos_t, sin_t)


def _diff_attn_kernel(lam_init, tk, group, q_ref, k_ref, v_ref, lq1_ref, lk1_ref, lq2_ref,
                      lk2_ref, sg_ref, o_ref, m_ref, acc_ref):
    tq = q_ref.shape[0]
    qi = pl.program_id(2)
    lane = lax.broadcasted_iota(jnp.int32, (tq, LANES), 1)
    q = q_ref[...]
    zero = jnp.zeros_like(q)
    qs = (jnp.where(lane < HEAD, q, zero), jnp.where(lane < HEAD, zero, q))
    lam = (jnp.exp(jnp.sum(lq1_ref[...] * lk1_ref[...], axis=-1, keepdims=True))
           - jnp.exp(jnp.sum(lq2_ref[...] * lk2_ref[...], axis=-1, keepdims=True)) + lam_init)

    def step(start, width, first):
        masked = first
        rows = pl.ds(pl.multiple_of(start, tk), width)
        kb = k_ref[rows, :]
        v_aug = jnp.concatenate([v_ref[rows, :], jnp.ones((width, LANES), BF16)], axis=1)
        s = [lax.dot_general(qs[c], kb, (_NT, ((), ())), preferred_element_type=F32)
             for c in range(2)]
        blocks = [[sc[:, b * LANES:(b + 1) * LANES] for b in range(width // LANES)] for sc in s]
        if masked:
            n_diag = tk // LANES
            qc = lax.broadcasted_iota(jnp.int32, (tq, LANES), 0) // ATTN_CHUNK
            kcol = lax.broadcasted_iota(jnp.int32, (tq, LANES), 1)
            for b in range(n_diag):
                allowed = (kcol + b * LANES) // ATTN_CHUNK <= qc
                for c in range(2):
                    blk = blocks[c][-n_diag + b]
                    blocks[c][-n_diag + b] = jnp.where(allowed, blk, NEG_BIG)
        m_old = [None if first else m_ref[c] for c in range(2)]
        m_new = []
        for c in range(2):
            mb = blocks[c][0]
            for blk in blocks[c][1:]:
                mb = jnp.maximum(mb, blk)
            mx = jnp.broadcast_to(jnp.max(mb, axis=-1, keepdims=True), (tq, LANES))
            m_new.append(mx if first else jnp.maximum(m_old[c], mx))
        p = [jnp.concatenate([jnp.exp2(blk - m_new[c]).astype(BF16) for blk in blocks[c]], axis=1)
             for c in range(2)]
        pv = [jnp.dot(p[c], v_aug, preferred_element_type=F32) for c in range(2)]
        for c in range(2):
            if first:
                acc_ref[c] = pv[c]
            else:
                alpha = jnp.exp2(m_old[c] - m_new[c])
                acc_ref[c] = jnp.concatenate([alpha, alpha], axis=1) * acc_ref[c] + pv[c]
            m_ref[c] = m_new[c]

    n_full = qi
    rem = n_full % group
    for r in range(group):
        @pl.when(rem == r)
        def _():
            step((n_full - r) * tk, (r + 1) * tk, True)

    def group_body(j, carry):
        step(j * (group * tk), group * tk, False)
        return carry

    lax.fori_loop(0, n_full // group, group_body, 0)

    a0 = acc_ref[0]
    a1 = acc_ref[1]
    o = a0[:, :LANES] / a0[:, LANES:] - lam * (a1[:, :LANES] / a1[:, LANES:])
    o = _rms(o, sg_ref[...], SUBLN_EPS) * (1.0 - lam_init)
    o_ref[...] = o.astype(o_ref.dtype)


def _diff_attn(q, k, v, lam_q1, lam_k1, lam_q2, lam_k2, subln_g, lam_init, batch, seq_len,
               tq=512, tk=512, group=6):
    t, d = q.shape
    assert seq_len % tq == 0 and tq == tk and tk % ATTN_CHUNK == 0
    n_q = seq_len // tq
    qtile = pl.BlockSpec((tq, LANES), lambda b, h, i: (b * n_q + i, h))
    kvfull = pl.BlockSpec((seq_len, LANES), lambda b, h, i: (b, h))
    small = lambda n: pl.BlockSpec((1, n), lambda b, h, i: (0, 0))
    return pl.pallas_call(
        functools.partial(_diff_attn_kernel, lam_init, tk, group),
        grid=(batch, d // LANES, n_q),
        in_specs=[qtile, kvfull, kvfull, small(HEAD), small(HEAD), small(HEAD), small(HEAD),
                  small(LANES)],
        out_specs=qtile,
        out_shape=jax.ShapeDtypeStruct((t, d), BF16),
        scratch_shapes=[pltpu.VMEM((2, tq, LANES), F32), pltpu.VMEM((2, tq, 2 * LANES), F32)],
        compiler_params=_cparams(("parallel", "parallel", "arbitrary")),
        name="diff_attn",
    )(q, k, v, lam_q1.reshape(1, HEAD), lam_k1.reshape(1, HEAD), lam_q2.reshape(1, HEAD),
      lam_k2.reshape(1, HEAD), subln_g.reshape(1, LANES))


def kernel(x, g_mix, g_ffn, rw_mu, rw_w_r, rw_w_k, rw_w_v, rw_w_o, rw_w0, rw_w1, rw_w2, rw_a0,
           rw_a1, rw_a2, rw_g1, rw_g2, rw_k_k, rw_k_a, rw_r_k, rw_lnx_w, rw_lnx_b, kv_g, kv_w,
           k_norm_g, da_w_q, da_q_norm_g, da_lam_q1, da_lam_k1, da_lam_q2, da_lam_k2,
           da_subln_g, da_w_o, ffn_w_gu, ffn_w_down):
    batch, seq_len, d = x.shape
    depth = g_mix.shape[0]
    n_rwkv = rw_mu.shape[0]
    x2 = x.reshape(batch * seq_len, d)
    k_sh = v_sh = None
    for layer in range(depth):
        if layer < n_rwkv:
            i = layer
            r, lw, k, v, kkn, a, g = _rwkv_pre(
                x2, seq_len, g_mix[layer], rw_mu[i], rw_w_r[i], rw_w_k[i], rw_w_v[i], rw_w0[i],
                rw_w1[i], rw_w2[i], rw_a0[i], rw_a1[i], rw_a2[i], rw_g1[i], rw_g2[i],
                rw_k_k[i], rw_k_a[i])
            yg = _rwkv_scan(r, lw, k, v, kkn, a, g, rw_r_k[i], rw_lnx_w[i], rw_lnx_b[i],
                            batch, seq_len)
            x2 = _proj_ffn(x2, yg, rw_w_o[i], g_ffn[layer], ffn_w_gu[layer],
                           ffn_w_down[layer], "proj_ffn_rwkv")
        else:
            j = layer - n_rwkv
            lam_init = 0.8 - 0.6 * math.exp(-0.3 * layer)
            q, k_new, v_new = _qkv(x2, seq_len, kv_g, g_mix[layer], kv_w, da_w_q[j],
                                   k_norm_g, da_q_norm_g[j])
            if j == 0:
                k_sh, v_sh = k_new, v_new
            o = _diff_attn(q, k_sh, v_sh, da_lam_q1[j], da_lam_k1[j], da_lam_q2[j],
                           da_lam_k2[j], da_subln_g[j], lam_init, batch, seq_len)
            x2 = _proj_ffn(x2, o, da_w_o[j], g_ffn[layer], ffn_w_gu[layer],
                           ffn_w_down[layer], "proj_ffn_attn")
    return x2.reshape(batch, seq_len, d)
```

```python
import functools
import math

import jax
import jax.numpy as jnp
from jax import lax
from jax.experimental import pallas as pl
from jax.experimental.pallas import tpu as pltpu

F32 = jnp.float32
BF16 = jnp.bfloat16
HIGHEST = lax.Precision.HIGHEST

LANES = 128
HEAD = 64
SCAN_CHUNK = 64
ATTN_CHUNK = 64
ROPE_DIM = 16
ROPE_THETA = 500000.0
NORM_EPS = 1e-6
LNX_EPS = 64e-5
SUBLN_EPS = 1e-5
NEG_BIG = -1e30
VMEM_LIMIT = 56 * 1024 * 1024


def _cparams(sem):
    return pltpu.CompilerParams(dimension_semantics=sem, vmem_limit_bytes=VMEM_LIMIT)


def _rms(x, g, eps=NORM_EPS):
    return x * lax.rsqrt(jnp.mean(x * x, axis=-1, keepdims=True) + eps) * g


def _bdot(a, b):
    return jnp.dot(a.astype(BF16), b.astype(BF16), preferred_element_type=F32)


def _group_matrix(n, group):
    r = lax.broadcasted_iota(jnp.int32, (n, n), 0) // group
    c = lax.broadcasted_iota(jnp.int32, (n, n), 1) // group
    return (r == c).astype(BF16)


def _group_sum(x, gmat):
    return jnp.dot(x.astype(BF16), gmat, preferred_element_type=F32)


def _rwkv_pre_kernel(seq_len, x_ref, xp_ref, g_ref, mu_ref, wr_ref, wk_ref, wv_ref,
                     w0_ref, w1_ref, w2_ref, a0_ref, a1_ref, a2_ref, g1_ref, g2_ref,
                     kk_ref, ka_ref,
                     r_out, lw_out, k_out, v_out, kkn_out, a_out, g_out):
    tm, d = x_ref.shape
    i = pl.program_id(0)
    g = g_ref[...]
    h = _rms(x_ref[...], g)
    hp = _rms(xp_ref[...], g)[7:8, :]
    hp = jnp.where((i * tm) % seq_len == 0, 0.0, hp)
    row = lax.broadcasted_iota(jnp.int32, (tm, d), 0)
    h_prev = jnp.where(row == 0, hp, pltpu.roll(h, 1, axis=0))
    hh = h_prev - h
    mu = mu_ref[...]
    mix = lambda n: h + hh * mu[n:n + 1]

    r_out[...] = _bdot(mix(0), wr_ref[...]).astype(r_out.dtype)
    v_out[...] = _bdot(mix(3), wv_ref[...]).astype(v_out.dtype)
    k = _bdot(mix(2), wk_ref[...])
    wlog = w0_ref[...] + _bdot(jnp.tanh(_bdot(mix(1), w1_ref[...])), w2_ref[...])
    lw_out[...] = -math.exp(-0.5) * jax.nn.sigmoid(wlog)
    a = jax.nn.sigmoid(a0_ref[...] + _bdot(_bdot(mix(4), a1_ref[...]), a2_ref[...]))
    a_out[...] = a.astype(a_out.dtype)
    g_out[...] = _bdot(jax.nn.sigmoid(_bdot(mix(5), g1_ref[...])), g2_ref[...]).astype(g_out.dtype)

    kk = k * kk_ref[...]
    gmat = _group_matrix(LANES, HEAD)
    for c in range(d // LANES):
        sl = slice(c * LANES, (c + 1) * LANES)
        ss = _group_sum(jnp.square(kk[:, sl]), gmat)
        kkn_out[:, sl] = (kk[:, sl] * lax.rsqrt(jnp.maximum(ss, 1e-24))).astype(kkn_out.dtype)
    k_out[...] = (k * (1.0 + (a - 1.0) * ka_ref[...])).astype(k_out.dtype)


def _pad_cols(w, n):
    return jnp.pad(w, ((0, 0), (0, n - w.shape[1])))


def _pad_rows(w, n):
    return jnp.pad(w, ((0, n - w.shape[0]), (0, 0)))


def _rwkv_pre(x2, seq_len, g_mix, mu, w_r, w_k, w_v, w0, w1, w2, a0, a1, a2, g1, g2, k_k, k_a,
              tm=512):
    t, d = x2.shape
    assert seq_len % tm == 0 and t % tm == 0
    lo_w = -(-w1.shape[1] // LANES) * LANES
    lo_a = -(-a1.shape[1] // LANES) * LANES
    lo_g = -(-g1.shape[1] // LANES) * LANES
    w1p, w2p = _pad_cols(w1, lo_w).astype(BF16), _pad_rows(w2, lo_w).astype(BF16)
    a1p, a2p = _pad_cols(a1, lo_a).astype(BF16), _pad_rows(a2, lo_a).astype(BF16)
    g1p, g2p = _pad_cols(g1, lo_g).astype(BF16), _pad_rows(g2, lo_g).astype(BF16)
    row = lambda v: v.reshape(1, d)
    const = lambda shape: pl.BlockSpec(shape, lambda i: (0, 0))
    tile = pl.BlockSpec((tm, d), lambda i: (i, 0))
    in_specs = [
        tile,
        pl.BlockSpec((8, d), lambda i: (jnp.maximum(i * (tm // 8) - 1, 0), 0)),
        const((1, d)), const((6, d)),
        const((d, d)), const((d, d)), const((d, d)),
        const((1, d)), const((d, lo_w)), const((lo_w, d)),
        const((1, d)), const((d, lo_a)), const((lo_a, d)),
        const((d, lo_g)), const((lo_g, d)),
        const((1, d)), const((1, d)),
    ]
    out_shape = [jax.ShapeDtypeStruct((t, d), F32 if n == 1 else BF16) for n in range(7)]
    return pl.pallas_call(
        functools.partial(_rwkv_pre_kernel, seq_len),
        grid=(t // tm,),
        in_specs=in_specs,
        out_specs=[tile] * 7,
        out_shape=out_shape,
        compiler_params=_cparams(("parallel",)),
        name="rwkv_pre",
    )(x2, x2, row(g_mix), mu, w_r.astype(BF16), w_k.astype(BF16), w_v.astype(BF16),
      row(w0), w1p, w2p, row(a0), a1p, a2p, g1p, g2p, row(k_k), row(k_a))


_NN = ((1,), (0,))
_NT = ((1,), (1,))
_TN = ((0,), (0,))


def _split_bf16(x):
    hi = x.astype(BF16)
    return hi, (x - hi.astype(F32)).astype(BF16)


def _mm(a, b, dims=_NN, passes=1):
    dg = lambda x, y: lax.dot_general(x, y, (dims, ((), ())), preferred_element_type=F32)
    if passes == 1:
        return dg(a.astype(BF16), b.astype(BF16))
    if passes == 3:
        ah, al = _split_bf16(a)
        bh, bl = _split_bf16(b)
        return dg(ah, bh) + (dg(ah, bl) + dg(al, bh))
    return lax.dot_general(a, b, (dims, ((), ())), preferred_element_type=F32, precision=HIGHEST)


_SCAN_PASSES = dict(a=1, t=1, s=1, z=1)


def _rwkv_scan_kernel(n_s, r_ref, lw_ref, k_ref, v_ref, kk_ref, a_ref, g_ref, rk_ref, lnw_ref,
                      lnb_ref, o_ref, z_ref, rm_ref, fy_ref, gc_ref, bonus_ref, gate_ref, y_ref):
    ts = r_ref.shape[0]
    L = SCAN_CHUNK
    L2 = 2 * L
    n_chunks = ts // L
    pa, pt, ps, pz = (_SCAN_PASSES[n] for n in "atsz")
    step = pl.program_id(0)
    cur = step % 2
    prev = 1 - cur

    @pl.when(step % n_s == 1)
    def _():
        z_ref[...] = jnp.zeros_like(z_ref)

    @pl.when(step == 0)
    def _():
        z_ref[...] = jnp.zeros_like(z_ref)
        rm_ref[1] = jnp.zeros(rm_ref.shape[1:], F32)
        fy_ref[1] = jnp.zeros(fy_ref.shape[1:], F32)
        gc_ref[1] = jnp.zeros(gc_ref.shape[1:], F32)
        bonus_ref[1] = jnp.zeros(bonus_ref.shape[1:], F32)
        gate_ref[1] = jnp.zeros(gate_ref.shape[1:], F32)

    head0 = lax.broadcasted_iota(jnp.int32, (L, LANES), 1) < HEAD
    r2 = lax.broadcasted_iota(jnp.int32, (L2, 2 * L2), 0)
    c2 = lax.broadcasted_iota(jnp.int32, (L2, 2 * L2), 1) % L2
    same = (r2 // L) == (c2 // L)
    m_incl = same & (c2 <= r2)
    m_strict = same & (c2 < r2)
    eye = (lax.broadcasted_iota(jnp.int32, (L2, L2), 0)
           == lax.broadcasted_iota(jnp.int32, (L2, L2), 1))
    eye_f = eye.astype(F32)
    zeros = jnp.zeros((L2, L2), F32)

    def stack(x):
        return jnp.concatenate([jnp.where(head0, x, 0.0), jnp.where(head0, 0.0, x)], axis=0)

    cs = range(n_chunks)
    sls = [slice(c * L, (c + 1) * L) for c in cs]

    state = [z_ref[...]]
    pending = list(cs)

    n_slots = 8
    slots_used = [0]

    def fold_slot():
        slots_used[0] += 1
        done_target = -(-n_chunks * slots_used[0] // n_slots)
        while pending and n_chunks - len(pending) < done_target:
            c = pending.pop(0)
            O = _mm(rm_ref[prev, c], state[0], _NN, pz)
            Y = O[:L2] + fy_ref[prev, c]
            state[0] = O[L2:] + gc_ref[prev, c]
            y_ref[sls[c], :] = Y[:L] + Y[L:]

    fold_slot()
    pos = lax.broadcasted_iota(jnp.int32, (L, LANES), 0)
    wl, At, Rt, V, Bt, Kt, AA, RR = ([] for _ in range(8))
    for c in cs:
        sl = sls[c]
        lw = lw_ref[sl, :]
        cum = lw
        shift = 1
        while shift < L:
            cum = cum + jnp.where(pos >= shift, pltpu.roll(cum, shift, axis=0), 0.0)
            shift *= 2
        e_pos = jnp.exp(cum)
        e_neg = jnp.exp(-cum)
        kk = kk_ref[sl, :].astype(F32)
        b = kk * a_ref[sl, :].astype(F32)
        wl.append(e_pos[L - 1:L, :])
        At.append(stack(-kk * jnp.exp(cum - lw)))
        Rt.append(stack(r_ref[sl, :].astype(F32) * e_pos))
        V.append(stack(v_ref[sl, :].astype(F32)))
        Bt.append(stack(b * e_neg))
        Kt.append(stack(k_ref[sl, :].astype(F32) * e_neg))
        BK = jnp.concatenate([Bt[c], Kt[c]], axis=0)
        AA.append(jnp.where(m_strict, _mm(At[c], BK, _NT, pa), 0.0))
        RR.append(jnp.where(m_incl, _mm(Rt[c], BK, _NT, pa), 0.0))
        if c in (n_chunks // 2 - 1, n_chunks - 1):
            fold_slot()
    AkV = [_mm(AA[c][:, L2:], V[c], _NN, ps) for c in cs]
    X = [AA[c][:, :L2] for c in cs]
    T = [eye_f + X[c] for c in cs]
    X = [_mm(X[c], X[c], _NN, pt) for c in cs]
    fold_slot()
    for _ in range(int(math.log2(L)) - 2):
        XT = [_mm(X[c], jnp.concatenate([X[c], T[c]], axis=1), _NN, pt) for c in cs]
        X = [XT[c][:, :L2] for c in cs]
        T = [T[c] + XT[c][:, L2:] for c in cs]
        fold_slot()
    assert not pending
    z_ref[...] = state[0]
    T = [T[c] + _mm(X[c], T[c], _NN, pt) for c in cs]
    PQ = [_mm(T[c], jnp.concatenate([At[c], AkV[c]], axis=1), _NN, ps) for c in cs]
    rhs = [jnp.concatenate([PQ[c], jnp.concatenate([zeros, V[c]], axis=1)], axis=0) for c in cs]

    gmat = _group_matrix(LANES, HEAD)
    y = y_ref[...]
    mean = _group_sum(y, gmat) * (1.0 / HEAD)
    yc = y - mean
    var = _group_sum(yc * yc, gmat) * (1.0 / HEAD)
    yn = yc * lax.rsqrt(var + LNX_EPS) * lnw_ref[...] + lnb_ref[...]
    o_ref[...] = ((yn + bonus_ref[prev]) * gate_ref[prev]).astype(o_ref.dtype)

    F = [_mm(RR[c], rhs[c], _NN, ps) for c in cs]
    G = [_mm(jnp.concatenate([Bt[c] * wl[c], Kt[c] * wl[c]], axis=0), rhs[c], _TN, ps)
         for c in cs]
    for c in cs:
        rm_ref[cur, c] = jnp.concatenate([Rt[c] + F[c][:, :L2],
                                          jnp.where(eye, wl[c], 0.0) + G[c][:, :L2]], axis=0)
        fy_ref[cur, c] = F[c][:, L2:]
        gc_ref[cur, c] = G[c][:, L2:]
    rk = r_ref[...].astype(F32) * k_ref[...].astype(F32) * rk_ref[...]
    bonus_ref[cur] = _group_sum(rk, gmat) * v_ref[...].astype(F32)
    gate_ref[cur] = g_ref[...].astype(F32)


def _rwkv_scan(r, lw, k, v, kkn, a, g, r_k, lnx_w, lnx_b, batch, seq_len, ts=1024):
    t, d = r.shape
    assert seq_len % ts == 0 and seq_len >= 2 * ts and ts % SCAN_CHUNK == 0 and d % LANES == 0
    n_s = seq_len // ts
    n_chunks = ts // SCAN_CHUNK
    n_pairs = d // LANES
    n_tiles = batch * n_pairs * n_s

    def tile_block(i):
        b, p, s = i // (n_pairs * n_s), (i // n_s) % n_pairs, i % n_s
        return b * n_s + s, p

    in_tile = pl.BlockSpec((ts, LANES), lambda i: tile_block(jnp.minimum(i, n_tiles - 1)))
    out_tile = pl.BlockSpec((ts, LANES), lambda i: tile_block(jnp.maximum(i - 1, 0)))
    vec_cur = pl.BlockSpec((1, LANES), lambda i: (0, tile_block(jnp.minimum(i, n_tiles - 1))[1]))
    vec_prev = pl.BlockSpec((1, LANES), lambda i: (0, tile_block(jnp.maximum(i - 1, 0))[1]))
    return pl.pallas_call(
        functools.partial(_rwkv_scan_kernel, n_s),
        grid=(n_tiles + 1,),
        in_specs=[in_tile] * 7 + [vec_cur, vec_prev, vec_prev],
        out_specs=out_tile,
        out_shape=jax.ShapeDtypeStruct((t, d), BF16),
        scratch_shapes=[pltpu.VMEM((LANES, LANES), F32),
                        pltpu.VMEM((2, n_chunks, 2 * LANES, LANES), F32),
                        pltpu.VMEM((2, n_chunks, LANES, LANES), F32),
                        pltpu.VMEM((2, n_chunks, LANES, LANES), F32),
                        pltpu.VMEM((2, ts, LANES), F32),
                        pltpu.VMEM((2, ts, LANES), F32),
                        pltpu.VMEM((ts, LANES), F32)],
        compiler_params=_cparams(("arbitrary",)),
        name="rwkv_scan",
    )(r, lw, k, v, kkn, a, g, r_k.reshape(1, d), lnx_w.reshape(1, d), lnx_b.reshape(1, d))


def _proj_ffn_kernel(tf, x_ref, a_ref, wo_ref, gf_ref, wgu_ref, wd_ref, o_ref, act_ref):
    ff = wd_ref.shape[0]
    x1 = x_ref[...] + jnp.dot(a_ref[...], wo_ref[...], preferred_element_type=F32)
    h = _rms(x1, gf_ref[...]).astype(BF16)
    for f in range(ff // tf):
        gate = jnp.dot(h, wgu_ref[:, f * tf:(f + 1) * tf], preferred_element_type=F32)
        up = jnp.dot(h, wgu_ref[:, ff + f * tf:ff + (f + 1) * tf], preferred_element_type=F32)
        act_ref[:, f * tf:(f + 1) * tf] = ((gate * jax.nn.sigmoid(gate)) * up).astype(BF16)
    o_ref[...] = x1 + jnp.dot(act_ref[...], wd_ref[...], preferred_element_type=F32)


def _proj_ffn(x2, a, w_o, g_ffn, w_gu, w_down, name, tm=512, tf=256):
    t, d = x2.shape
    ff = w_down.shape[0]
    assert t % tm == 0 and ff % tf == 0
    tile = pl.BlockSpec((tm, d), lambda i: (i, 0))
    resident = lambda shape: pl.BlockSpec(shape, lambda i: (0, 0), pipeline_mode=pl.Buffered(1))
    return pl.pallas_call(
        functools.partial(_proj_ffn_kernel, tf),
        grid=(t // tm,),
        in_specs=[tile, tile, resident((d, d)), resident((1, d)), resident((d, 2 * ff)),
                  resident((ff, d))],
        out_specs=tile,
        out_shape=jax.ShapeDtypeStruct((t, d), F32),
        scratch_shapes=[pltpu.VMEM((tm, ff), BF16)],
        compiler_params=_cparams(("parallel",)),
        name=name,
    )(x2, a, w_o.astype(BF16), g_ffn.reshape(1, d), w_gu.astype(BF16), w_down.astype(BF16))


def _head_sumsq(x, gmat):
    return [_group_sum(jnp.square(x[:, c * LANES:(c + 1) * LANES]), gmat)
            for c in range(x.shape[1] // LANES)]


def _head_norm_rope(x, sumsq, gain, cos, sin, out_ref, scale):
    d = x.shape[1]
    lane = lax.broadcasted_iota(jnp.int32, (x.shape[0], LANES), 1) % HEAD
    half = ROPE_DIM // 2
    for c in range(d // LANES):
        sl = slice(c * LANES, (c + 1) * LANES)
        xc = x[:, sl]
        ms = sumsq[c] * (1.0 / HEAD)
        y = xc * lax.rsqrt(ms + NORM_EPS) * gain[:, sl]
        partner = jnp.where(lane < half, pltpu.roll(y, LANES - half, axis=1),
                            pltpu.roll(y, half, axis=1))
        out = y * cos + partner * sin
        if scale != 1.0:
            out = out * scale
        out_ref[:, sl] = out.astype(out_ref.dtype)


def _qkv_kernel(scale, x_ref, gkv_ref, gq_ref, wkv_ref, wq_ref, kn_ref, qn_ref, cos_ref, sin_ref,
                q_out, k_out, v_out):
    d = x_ref.shape[1]
    x = x_ref[...]
    inv = lax.rsqrt(jnp.mean(x * x, axis=-1, keepdims=True) + NORM_EPS)
    xn = x * inv
    hkv = (xn * gkv_ref[...]).astype(BF16)
    hq = (xn * gq_ref[...]).astype(BF16)
    gmat = _group_matrix(LANES, HEAD)
    cos = cos_ref[...]
    sin = sin_ref[...]
    k = jnp.dot(hkv, wkv_ref[:, :d], preferred_element_type=F32)
    q = jnp.dot(hq, wq_ref[...], preferred_element_type=F32)
    k_sumsq = _head_sumsq(k, gmat)
    _head_norm_rope(k, k_sumsq, kn_ref[...], cos, sin, k_out, 1.0)
    q_sumsq = _head_sumsq(q, gmat)
    v_out[...] = jnp.dot(hkv, wkv_ref[:, d:], preferred_element_type=F32).astype(v_out.dtype)
    _head_norm_rope(q, q_sumsq, qn_ref[...], cos, sin, q_out, scale)


def _rope_tables(seq_len):
    half = ROPE_DIM // 2
    pos = jnp.arange(seq_len, dtype=F32)
    inv = jnp.power(ROPE_THETA, -jnp.arange(0, ROPE_DIM, 2, dtype=F32) / ROPE_DIM)
    ang = pos[:, None] * inv[None, :]
    cos, sin = jnp.cos(ang), jnp.sin(ang)
    ones = jnp.ones((seq_len, HEAD - ROPE_DIM), F32)
    cos_h = jnp.concatenate([cos, cos, ones], axis=1)
    sin_h = jnp.concatenate([-sin, sin, 0.0 * ones], axis=1)
    reps = LANES // HEAD
    return jnp.tile(cos_h, (1, reps)), jnp.tile(sin_h, (1, reps))


def _qkv(x2, seq_len, kv_g, g_q, kv_w, w_q, k_norm_g, q_norm_g, tm=512):
    t, d = x2.shape
    assert seq_len % tm == 0
    n_s = seq_len // tm
    cos_t, sin_t = _rope_tables(seq_len)
    heads = d // HEAD
    scale = HEAD ** -0.5 * math.log2(math.e)
    tile = pl.BlockSpec((tm, d), lambda i: (i, 0))
    const = lambda shape: pl.BlockSpec(shape, lambda i: (0, 0))
    tab = pl.BlockSpec((tm, LANES), lambda i: (i % n_s, 0))
    return pl.pallas_call(
        functools.partial(_qkv_kernel, scale),
        grid=(t // tm,),
        in_specs=[tile, const((1, d)), const((1, d)), const((d, 2 * d)), const((d, d)),
                  const((1, d)), const((1, d)), tab, tab],
        out_specs=[tile] * 3,
        out_shape=[jax.ShapeDtypeStruct((t, d), BF16)] * 3,
        compiler_params=_cparams(("parallel",)),
        name="qkv",
    )(x2, kv_g.reshape(1, d), g_q.reshape(1, d), kv_w.astype(BF16), w_q.astype(BF16),
      jnp.tile(k_norm_g, heads).reshape(1, d), jnp.tile(q_norm_g, heads).reshape(1, d),
      cos_t, sin_t)


def _diff_attn_kernel(lam_init, tk, group, q_ref, k_ref, v_ref, lq1_ref, lk1_ref, lq2_ref,
                      lk2_ref, sg_ref, o_ref, m_ref, acc_ref):
    tq = q_ref.shape[0]
    qi = pl.program_id(2)
    lane = lax.broadcasted_iota(jnp.int32, (tq, LANES), 1)
    q = q_ref[...]
    zero = jnp.zeros_like(q)
    qs = (jnp.where(lane < HEAD, q, zero), jnp.where(lane < HEAD, zero, q))
    lam = (jnp.exp(jnp.sum(lq1_ref[...] * lk1_ref[...], axis=-1, keepdims=True))
           - jnp.exp(jnp.sum(lq2_ref[...] * lk2_ref[...], axis=-1, keepdims=True)) + lam_init)

    def step(start, width, first):
        masked = first
        rows = pl.ds(pl.multiple_of(start, tk), width)
        kb = k_ref[rows, :]
        v_aug = jnp.concatenate([v_ref[rows, :], jnp.ones((width, LANES), BF16)], axis=1)
        s = [lax.dot_general(qs[c], kb, (_NT, ((), ())), preferred_element_type=F32)
             for c in range(2)]
        blocks = [[sc[:, b * LANES:(b + 1) * LANES] for b in range(width // LANES)] for sc in s]
        if masked:
            n_diag = tk // LANES
            qc = lax.broadcasted_iota(jnp.int32, (tq, LANES), 0) // ATTN_CHUNK
            kcol = lax.broadcasted_iota(jnp.int32, (tq, LANES), 1)
            for b in range(n_diag):
                allowed = (kcol + b * LANES) // ATTN_CHUNK <= qc
                for c in range(2):
                    blk = blocks[c][-n_diag + b]
                    blocks[c][-n_diag + b] = jnp.where(allowed, blk, NEG_BIG)
        m_old = [None if first else m_ref[c] for c in range(2)]
        m_new = []
        for c in range(2):
            mb = blocks[c][0]
            for blk in blocks[c][1:]:
                mb = jnp.maximum(mb, blk)
            mx = jnp.broadcast_to(jnp.max(mb, axis=-1, keepdims=True), (tq, LANES))
            m_new.append(mx if first else jnp.maximum(m_old[c], mx))
        p = [jnp.concatenate([jnp.exp2(blk - m_new[c]).astype(BF16) for blk in blocks[c]], axis=1)
             for c in range(2)]
        pv = [jnp.dot(p[c], v_aug, preferred_element_type=F32) for c in range(2)]
        for c in range(2):
            if first:
                acc_ref[c] = pv[c]
            else:
                alpha = jnp.exp2(m_old[c] - m_new[c])
                acc_ref[c] = jnp.concatenate([alpha, alpha], axis=1) * acc_ref[c] + pv[c]
            m_ref[c] = m_new[c]

    n_full = qi
    rem = n_full % group
    for r in range(group):
        @pl.when(rem == r)
        def _():
            step((n_full - r) * tk, (r + 1) * tk, True)

    def group_body(j, carry):
        step(j * (group * tk), group * tk, False)
        return carry

    lax.fori_loop(0, n_full // group, group_body, 0)

    a0 = acc_ref[0]
    a1 = acc_ref[1]
    o = a0[:, :LANES] / a0[:, LANES:] - lam * (a1[:, :LANES] / a1[:, LANES:])
    o = _rms(o, sg_ref[...], SUBLN_EPS) * (1.0 - lam_init)
    o_ref[...] = o.astype(o_ref.dtype)


def _diff_attn(q, k, v, lam_q1, lam_k1, lam_q2, lam_k2, subln_g, lam_init, batch, seq_len,
               tq=512, tk=512, group=6):
    t, d = q.shape
    assert seq_len % tq == 0 and tq == tk and tk % ATTN_CHUNK == 0
    n_q = seq_len // tq
    qtile = pl.BlockSpec((tq, LANES), lambda b, h, i: (b * n_q + i, h))
    kvfull = pl.BlockSpec((seq_len, LANES), lambda b, h, i: (b, h))
    small = lambda n: pl.BlockSpec((1, n), lambda b, h, i: (0, 0))
    return pl.pallas_call(
        functools.partial(_diff_attn_kernel, lam_init, tk, group),
        grid=(batch, d // LANES, n_q),
        in_specs=[qtile, kvfull, kvfull, small(HEAD), small(HEAD), small(HEAD), small(HEAD),
                  small(LANES)],
        out_specs=qtile,
        out_shape=jax.ShapeDtypeStruct((t, d), BF16),
        scratch_shapes=[pltpu.VMEM((2, tq, LANES), F32), pltpu.VMEM((2, tq, 2 * LANES), F32)],
        compiler_params=_cparams(("parallel", "parallel", "arbitrary")),
        name="diff_attn",
    )(q, k, v, lam_q1.reshape(1, HEAD), lam_k1.reshape(1, HEAD), lam_q2.reshape(1, HEAD),
      lam_k2.reshape(1, HEAD), subln_g.reshape(1, LANES))


def kernel(x, g_mix, g_ffn, rw_mu, rw_w_r, rw_w_k, rw_w_v, rw_w_o, rw_w0, rw_w1, rw_w2, rw_a0,
           rw_a1, rw_a2, rw_g1, rw_g2, rw_k_k, rw_k_a, rw_r_k, rw_lnx_w, rw_lnx_b, kv_g, kv_w,
           k_norm_g, da_w_q, da_q_norm_g, da_lam_q1, da_lam_k1, da_lam_q2, da_lam_k2,
           da_subln_g, da_w_o, ffn_w_gu, ffn_w_down):
    batch, seq_len, d = x.shape
    depth = g_mix.shape[0]
    n_rwkv = rw_mu.shape[0]
    x2 = x.reshape(batch * seq_len, d)
    k_sh = v_sh = None
    for layer in range(depth):
        if layer < n_rwkv:
            i = layer
            r, lw, k, v, kkn, a, g = _rwkv_pre(
                x2, seq_len, g_mix[layer], rw_mu[i], rw_w_r[i], rw_w_k[i], rw_w_v[i], rw_w0[i],
                rw_w1[i], rw_w2[i], rw_a0[i], rw_a1[i], rw_a2[i], rw_g1[i], rw_g2[i],
                rw_k_k[i], rw_k_a[i])
            yg = _rwkv_scan(r, lw, k, v, kkn, a, g, rw_r_k[i], rw_lnx_w[i], rw_lnx_b[i],
                            batch, seq_len)
            x2 = _proj_ffn(x2, yg, rw_w_o[i], g_ffn[layer], ffn_w_gu[layer],
                           ffn_w_down[layer], "proj_ffn_rwkv")
        else:
            j = layer - n_rwkv
            lam_init = 0.8 - 0.6 * math.exp(-0.3 * layer)
            q, k_new, v_new = _qkv(x2, seq_len, kv_g, g_mix[layer], kv_w, da_w_q[j],
                                   k_norm_g, da_q_norm_g[j])
            if j == 0:
                k_sh, v_sh = k_new, v_new
            o = _diff_attn(q, k_sh, v_sh, da_lam_q1[j], da_lam_k1[j], da_lam_q2[j],
                           da_lam_k2[j], da_subln_g[j], lam_init, batch, seq_len)
            x2 = _proj_ffn(x2, o, da_w_o[j], g_ffn[layer], ffn_w_gu[layer],
                           ffn_w_down[layer], "proj_ffn_attn")
    return x2.reshape(batch, seq_len, d)
```

```python
import functools
import math

import jax
import jax.numpy as jnp
from jax import lax
from jax.experimental import pallas as pl
from jax.experimental.pallas import tpu as pltpu

F32 = jnp.float32
BF16 = jnp.bfloat16

LANES = 128
HEAD = 64
SCAN_CHUNK = 64
ATTN_CHUNK = 64
ROPE_DIM = 16
ROPE_THETA = 500000.0
NORM_EPS = 1e-6
LNX_EPS = 64e-5
SUBLN_EPS = 1e-5
NEG_BIG = -1e30
VMEM_LIMIT = 56 * 1024 * 1024


def _cparams(sem):
    return pltpu.CompilerParams(dimension_semantics=sem, vmem_limit_bytes=VMEM_LIMIT)


def _rms(x, g, eps=NORM_EPS):
    return x * lax.rsqrt(jnp.mean(x * x, axis=-1, keepdims=True) + eps) * g


def _bdot(a, b):
    return jnp.dot(a.astype(BF16), b.astype(BF16), preferred_element_type=F32)


def _group_matrix(n, group):
    r = lax.broadcasted_iota(jnp.int32, (n, n), 0) // group
    c = lax.broadcasted_iota(jnp.int32, (n, n), 1) // group
    return (r == c).astype(BF16)


def _group_sum(x, gmat):
    return jnp.dot(x.astype(BF16), gmat, preferred_element_type=F32)


def _rwkv_pre_kernel(seq_len, x_ref, xp_ref, g_ref, mu_ref, wr_ref, wk_ref, wv_ref,
                     w0_ref, w1_ref, w2_ref, a0_ref, a1_ref, a2_ref, g1_ref, g2_ref,
                     kk_ref, ka_ref,
                     r_out, lw_out, k_out, v_out, kkn_out, a_out, g_out):
    tm, d = x_ref.shape
    i = pl.program_id(0)
    g = g_ref[...]
    h = _rms(x_ref[...], g)
    hp = _rms(xp_ref[...], g)[7:8, :]
    hp = jnp.where((i * tm) % seq_len == 0, 0.0, hp)
    row = lax.broadcasted_iota(jnp.int32, (tm, d), 0)
    h_prev = jnp.where(row == 0, hp, pltpu.roll(h, 1, axis=0))
    hh = h_prev - h
    mu = mu_ref[...]
    mix = lambda n: h + hh * mu[n:n + 1]

    r_out[...] = _bdot(mix(0), wr_ref[...]).astype(r_out.dtype)
    v_out[...] = _bdot(mix(3), wv_ref[...]).astype(v_out.dtype)
    k = _bdot(mix(2), wk_ref[...])
    wlog = w0_ref[...] + _bdot(jnp.tanh(_bdot(mix(1), w1_ref[...])), w2_ref[...])
    lw_out[...] = -math.exp(-0.5) * jax.nn.sigmoid(wlog)
    a = jax.nn.sigmoid(a0_ref[...] + _bdot(_bdot(mix(4), a1_ref[...]), a2_ref[...]))
    a_out[...] = a.astype(a_out.dtype)
    g_out[...] = _bdot(jax.nn.sigmoid(_bdot(mix(5), g1_ref[...])), g2_ref[...]).astype(g_out.dtype)

    kk = k * kk_ref[...]
    gmat = _group_matrix(LANES, HEAD)
    for c in range(d // LANES):
        sl = slice(c * LANES, (c + 1) * LANES)
        ss = _group_sum(jnp.square(kk[:, sl]), gmat)
        kkn_out[:, sl] = (kk[:, sl] * lax.rsqrt(jnp.maximum(ss, 1e-24))).astype(kkn_out.dtype)
    k_out[...] = (k * (1.0 + (a - 1.0) * ka_ref[...])).astype(k_out.dtype)


def _pad_cols(w, n):
    return jnp.pad(w, ((0, 0), (0, n - w.shape[1])))


def _pad_rows(w, n):
    return jnp.pad(w, ((0, n - w.shape[0]), (0, 0)))


def _rwkv_pre(x2, seq_len, g_mix, mu, w_r, w_k, w_v, w0, w1, w2, a0, a1, a2, g1, g2, k_k, k_a,
              tm=512):
    t, d = x2.shape
    assert seq_len % tm == 0 and t % tm == 0
    lo_w = -(-w1.shape[1] // LANES) * LANES
    lo_a = -(-a1.shape[1] // LANES) * LANES
    lo_g = -(-g1.shape[1] // LANES) * LANES
    w1p, w2p = _pad_cols(w1, lo_w).astype(BF16), _pad_rows(w2, lo_w).astype(BF16)
    a1p, a2p = _pad_cols(a1, lo_a).astype(BF16), _pad_rows(a2, lo_a).astype(BF16)
    g1p, g2p = _pad_cols(g1, lo_g).astype(BF16), _pad_rows(g2, lo_g).astype(BF16)
    row = lambda v: v.reshape(1, d)
    const = lambda shape: pl.BlockSpec(shape, lambda i: (0, 0))
    tile = pl.BlockSpec((tm, d), lambda i: (i, 0))
    in_specs = [
        tile,
        pl.BlockSpec((8, d), lambda i: (jnp.maximum(i * (tm // 8) - 1, 0), 0)),
        const((1, d)), const((6, d)),
        const((d, d)), const((d, d)), const((d, d)),
        const((1, d)), const((d, lo_w)), const((lo_w, d)),
        const((1, d)), const((d, lo_a)), const((lo_a, d)),
        const((d, lo_g)), const((lo_g, d)),
        const((1, d)), const((1, d)),
    ]
    out_shape = [jax.ShapeDtypeStruct((t, d), F32 if n == 1 else BF16) for n in range(7)]
    return pl.pallas_call(
        functools.partial(_rwkv_pre_kernel, seq_len),
        grid=(t // tm,),
        in_specs=in_specs,
        out_specs=[tile] * 7,
        out_shape=out_shape,
        compiler_params=_cparams(("parallel",)),
        name="rwkv_pre",
    )(x2, x2, row(g_mix), mu, w_r.astype(BF16), w_k.astype(BF16), w_v.astype(BF16),
      row(w0), w1p, w2p, row(a0), a1p, a2p, g1p, g2p, row(k_k), row(k_a))


_NN = ((1,), (0,))
_NT = ((1,), (1,))
_TN = ((0,), (0,))


def _mm(a, b, dims=_NN):
    return lax.dot_general(a.astype(BF16), b.astype(BF16), (dims, ((), ())),
                           preferred_element_type=F32)


def _rwkv_scan_kernel(n_s, r_ref, lw_ref, k_ref, v_ref, kk_ref, a_ref, g_ref, rk_ref, lnw_ref,
                      lnb_ref, o_ref, z_ref, rm_ref, fy_ref, gc_ref, bonus_ref, gate_ref, y_ref):
    ts = r_ref.shape[0]
    L = SCAN_CHUNK
    L2 = 2 * L
    n_chunks = ts // L
    step = pl.program_id(0)
    cur = step % 2
    prev = 1 - cur

    @pl.when(step % n_s == 1)
    def _():
        z_ref[...] = jnp.zeros_like(z_ref)

    @pl.when(step == 0)
    def _():
        z_ref[...] = jnp.zeros_like(z_ref)
        rm_ref[1] = jnp.zeros(rm_ref.shape[1:], F32)
        fy_ref[1] = jnp.zeros(fy_ref.shape[1:], F32)
        gc_ref[1] = jnp.zeros(gc_ref.shape[1:], F32)
        bonus_ref[1] = jnp.zeros(bonus_ref.shape[1:], F32)
        gate_ref[1] = jnp.zeros(gate_ref.shape[1:], F32)

    head0 = lax.broadcasted_iota(jnp.int32, (L, LANES), 1) < HEAD
    r2 = lax.broadcasted_iota(jnp.int32, (L2, 2 * L2), 0)
    c2 = lax.broadcasted_iota(jnp.int32, (L2, 2 * L2), 1) % L2
    same = (r2 // L) == (c2 // L)
    m_incl = same & (c2 <= r2)
    m_strict = same & (c2 < r2)
    eye = (lax.broadcasted_iota(jnp.int32, (L2, L2), 0)
           == lax.broadcasted_iota(jnp.int32, (L2, L2), 1))
    eye_f = eye.astype(F32)
    zeros = jnp.zeros((L2, L2), F32)

    def stack(x):
        return jnp.concatenate([jnp.where(head0, x, 0.0), jnp.where(head0, 0.0, x)], axis=0)

    cs = range(n_chunks)
    sls = [slice(c * L, (c + 1) * L) for c in cs]

    state = [z_ref[...]]
    pending = list(cs)

    n_slots = 8
    slots_used = [0]

    def fold_slot():
        slots_used[0] += 1
        done_target = -(-n_chunks * slots_used[0] // n_slots)
        while pending and n_chunks - len(pending) < done_target:
            c = pending.pop(0)
            O = _mm(rm_ref[prev, c], state[0])
            Y = O[:L2] + fy_ref[prev, c]
            state[0] = O[L2:] + gc_ref[prev, c]
            y_ref[sls[c], :] = Y[:L] + Y[L:]

    fold_slot()
    pos = lax.broadcasted_iota(jnp.int32, (L, LANES), 0)
    wl, At, Rt, V, Bt, Kt, AA, RR = ([] for _ in range(8))
    for c in cs:
        sl = sls[c]
        lw = lw_ref[sl, :]
        cum = lw
        shift = 1
        while shift < L:
            cum = cum + jnp.where(pos >= shift, pltpu.roll(cum, shift, axis=0), 0.0)
            shift *= 2
        e_pos = jnp.exp(cum)
        e_neg = jnp.exp(-cum)
        kk = kk_ref[sl, :].astype(F32)
        b = kk * a_ref[sl, :].astype(F32)
        wl.append(e_pos[L - 1:L, :])
        At.append(stack(-kk * jnp.exp(cum - lw)))
        Rt.append(stack(r_ref[sl, :].astype(F32) * e_pos))
        V.append(stack(v_ref[sl, :].astype(F32)))
        Bt.append(stack(b * e_neg))
        Kt.append(stack(k_ref[sl, :].astype(F32) * e_neg))
        BK = jnp.concatenate([Bt[c], Kt[c]], axis=0)
        AA.append(jnp.where(m_strict, _mm(At[c], BK, _NT), 0.0))
        RR.append(jnp.where(m_incl, _mm(Rt[c], BK, _NT), 0.0))
        if c in (n_chunks // 2 - 1, n_chunks - 1):
            fold_slot()
    AkV = [_mm(AA[c][:, L2:], V[c]) for c in cs]
    X = [AA[c][:, :L2] for c in cs]
    T = [eye_f + X[c] for c in cs]
    X = [_mm(X[c], X[c]) for c in cs]
    fold_slot()
    for _ in range(int(math.log2(L)) - 2):
        XT = [_mm(X[c], jnp.concatenate([X[c], T[c]], axis=1)) for c in cs]
        X = [XT[c][:, :L2] for c in cs]
        T = [T[c] + XT[c][:, L2:] for c in cs]
        fold_slot()
    assert not pending
    z_ref[...] = state[0]
    T = [T[c] + _mm(X[c], T[c]) for c in cs]
    PQ = [_mm(T[c], jnp.concatenate([At[c], AkV[c]], axis=1)) for c in cs]
    rhs = [jnp.concatenate([PQ[c], jnp.concatenate([zeros, V[c]], axis=1)], axis=0) for c in cs]

    gmat = _group_matrix(LANES, HEAD)
    y = y_ref[...]
    mean = _group_sum(y, gmat) * (1.0 / HEAD)
    yc = y - mean
    var = _group_sum(yc * yc, gmat) * (1.0 / HEAD)
    yn = yc * lax.rsqrt(var + LNX_EPS) * lnw_ref[...] + lnb_ref[...]
    o_ref[...] = ((yn + bonus_ref[prev]) * gate_ref[prev]).astype(o_ref.dtype)

    F = [_mm(RR[c], rhs[c]) for c in cs]
    G = [_mm(jnp.concatenate([Bt[c] * wl[c], Kt[c] * wl[c]], axis=0), rhs[c], _TN)
         for c in cs]
    for c in cs:
        rm_ref[cur, c] = jnp.concatenate([Rt[c] + F[c][:, :L2],
                                          jnp.where(eye, wl[c], 0.0) + G[c][:, :L2]], axis=0)
        fy_ref[cur, c] = F[c][:, L2:]
        gc_ref[cur, c] = G[c][:, L2:]
    rk = r_ref[...].astype(F32) * k_ref[...].astype(F32) * rk_ref[...]
    bonus_ref[cur] = _group_sum(rk, gmat) * v_ref[...].astype(F32)
    gate_ref[cur] = g_ref[...].astype(F32)


def _rwkv_scan(r, lw, k, v, kkn, a, g, r_k, lnx_w, lnx_b, batch, seq_len, ts=1024):
    t, d = r.shape
    assert seq_len % ts == 0 and seq_len >= 2 * ts and ts % SCAN_CHUNK == 0 and d % LANES == 0
    n_s = seq_len // ts
    n_chunks = ts // SCAN_CHUNK
    n_pairs = d // LANES
    n_tiles = batch * n_pairs * n_s

    def tile_block(i):
        b, p, s = i // (n_pairs * n_s), (i // n_s) % n_pairs, i % n_s
        return b * n_s + s, p

    in_tile = pl.BlockSpec((ts, LANES), lambda i: tile_block(jnp.minimum(i, n_tiles - 1)))
    out_tile = pl.BlockSpec((ts, LANES), lambda i: tile_block(jnp.maximum(i - 1, 0)))
    vec_cur = pl.BlockSpec((1, LANES), lambda i: (0, tile_block(jnp.minimum(i, n_tiles - 1))[1]))
    vec_prev = pl.BlockSpec((1, LANES), lambda i: (0, tile_block(jnp.maximum(i - 1, 0))[1]))
    return pl.pallas_call(
        functools.partial(_rwkv_scan_kernel, n_s),
        grid=(n_tiles + 1,),
        in_specs=[in_tile] * 7 + [vec_cur, vec_prev, vec_prev],
        out_specs=out_tile,
        out_shape=jax.ShapeDtypeStruct((t, d), BF16),
        scratch_shapes=[pltpu.VMEM((LANES, LANES), F32),
                        pltpu.VMEM((2, n_chunks, 2 * LANES, LANES), F32),
                        pltpu.VMEM((2, n_chunks, LANES, LANES), F32),
                        pltpu.VMEM((2, n_chunks, LANES, LANES), F32),
                        pltpu.VMEM((2, ts, LANES), F32),
                        pltpu.VMEM((2, ts, LANES), F32),
                        pltpu.VMEM((ts, LANES), F32)],
        compiler_params=_cparams(("arbitrary",)),
        name="rwkv_scan",
    )(r, lw, k, v, kkn, a, g, r_k.reshape(1, d), lnx_w.reshape(1, d), lnx_b.reshape(1, d))


def _proj_ffn_kernel(tf, x_ref, a_ref, wo_ref, gf_ref, wgu_ref, wd_ref, o_ref, act_ref):
    ff = wd_ref.shape[0]
    x1 = x_ref[...] + jnp.dot(a_ref[...], wo_ref[...], preferred_element_type=F32)
    h = _rms(x1, gf_ref[...]).astype(BF16)
    for f in range(ff // tf):
        gate = jnp.dot(h, wgu_ref[:, f * tf:(f + 1) * tf], preferred_element_type=F32)
        up = jnp.dot(h, wgu_ref[:, ff + f * tf:ff + (f + 1) * tf], preferred_element_type=F32)
        act_ref[:, f * tf:(f + 1) * tf] = ((gate * jax.nn.sigmoid(gate)) * up).astype(BF16)
    o_ref[...] = x1 + jnp.dot(act_ref[...], wd_ref[...], preferred_element_type=F32)


def _proj_ffn(x2, a, w_o, g_ffn, w_gu, w_down, name, tm=512, tf=256):
    t, d = x2.shape
    ff = w_down.shape[0]
    assert t % tm == 0 and ff % tf == 0
    tile = pl.BlockSpec((tm, d), lambda i: (i, 0))
    resident = lambda shape: pl.BlockSpec(shape, lambda i: (0, 0), pipeline_mode=pl.Buffered(1))
    return pl.pallas_call(
        functools.partial(_proj_ffn_kernel, tf),
        grid=(t // tm,),
        in_specs=[tile, tile, resident((d, d)), resident((1, d)), resident((d, 2 * ff)),
                  resident((ff, d))],
        out_specs=tile,
        out_shape=jax.ShapeDtypeStruct((t, d), F32),
        scratch_shapes=[pltpu.VMEM((tm, ff), BF16)],
        compiler_params=_cparams(("parallel",)),
        name=name,
    )(x2, a, w_o.astype(BF16), g_ffn.reshape(1, d), w_gu.astype(BF16), w_down.astype(BF16))


def _head_sumsq(x, gmat):
    return [_group_sum(jnp.square(x[:, c * LANES:(c + 1) * LANES]), gmat)
            for c in range(x.shape[1] // LANES)]


def _head_norm_rope(x, sumsq, gain, cos, sin, out_ref, scale):
    d = x.shape[1]
    lane = lax.broadcasted_iota(jnp.int32, (x.shape[0], LANES), 1) % HEAD
    half = ROPE_DIM // 2
    for c in range(d // LANES):
        sl = slice(c * LANES, (c + 1) * LANES)
        xc = x[:, sl]
        ms = sumsq[c] * (1.0 / HEAD)
        y = xc * lax.rsqrt(ms + NORM_EPS) * gain[:, sl]
        partner = jnp.where(lane < half, pltpu.roll(y, LANES - half, axis=1),
                            pltpu.roll(y, half, axis=1))
        out = y * cos + partner * sin
        if scale != 1.0:
            out = out * scale
        out_ref[:, sl] = out.astype(out_ref.dtype)


def _qkv_kernel(scale, x_ref, gkv_ref, gq_ref, wkv_ref, wq_ref, kn_ref, qn_ref, cos_ref, sin_ref,
                q_out, k_out, v_out):
    d = x_ref.shape[1]
    x = x_ref[...]
    inv = lax.rsqrt(jnp.mean(x * x, axis=-1, keepdims=True) + NORM_EPS)
    xn = x * inv
    hkv = (xn * gkv_ref[...]).astype(BF16)
    hq = (xn * gq_ref[...]).astype(BF16)
    gmat = _group_matrix(LANES, HEAD)
    cos = cos_ref[...]
    sin = sin_ref[...]
    k = jnp.dot(hkv, wkv_ref[:, :d], preferred_element_type=F32)
    q = jnp.dot(hq, wq_ref[...], preferred_element_type=F32)
    k_sumsq = _head_sumsq(k, gmat)
    _head_norm_rope(k, k_sumsq, kn_ref[...], cos, sin, k_out, 1.0)
    q_sumsq = _head_sumsq(q, gmat)
    v_out[...] = jnp.dot(hkv, wkv_ref[:, d:], preferred_element_type=F32).astype(v_out.dtype)
    _head_norm_rope(q, q_sumsq, qn_ref[...], cos, sin, q_out, scale)


def _rope_tables(seq_len):
    half = ROPE_DIM // 2
    pos = jnp.arange(seq_len, dtype=F32)
    inv = jnp.power(ROPE_THETA, -jnp.arange(0, ROPE_DIM, 2, dtype=F32) / ROPE_DIM)
    ang = pos[:, None] * inv[None, :]
    cos, sin = jnp.cos(ang), jnp.sin(ang)
    ones = jnp.ones((seq_len, HEAD - ROPE_DIM), F32)
    cos_h = jnp.concatenate([cos, cos, ones], axis=1)
    sin_h = jnp.concatenate([-sin, sin, 0.0 * ones], axis=1)
    reps = LANES // HEAD
    return jnp.tile(cos_h, (1, reps)), jnp.tile(sin_h, (1, reps))


def _qkv(x2, seq_len, kv_g, g_q, kv_w, w_q, k_norm_g, q_norm_g, tm=512):
    t, d = x2.shape
    assert seq_len % tm == 0
    n_s = seq_len // tm
    cos_t, sin_t = _rope_tables(seq_len)
    heads = d // HEAD
    scale = HEAD ** -0.5 * math.log2(math.e)
    tile = pl.BlockSpec((tm, d), lambda i: (i, 0))
    const = lambda shape: pl.BlockSpec(shape, lambda i: (0, 0))
    tab = pl.BlockSpec((tm, LANES), lambda i: (i % n_s, 0))
    return pl.pallas_call(
        functools.partial(_qkv_kernel, scale),
        grid=(t // tm,),
        in_specs=[tile, const((1, d)), const((1, d)), const((d, 2 * d)), const((d, d)),
                  const((1, d)), const((1, d)), tab, tab],
        out_specs=[tile] * 3,
        out_shape=[jax.ShapeDtypeStruct((t, d), BF16)] * 3,
        compiler_params=_cparams(("parallel",)),
        name="qkv",
    )(x2, kv_g.reshape(1, d), g_q.reshape(1, d), kv_w.astype(BF16), w_q.astype(BF16),
      jnp.tile(k_norm_g, heads).reshape(1, d), jnp.tile(q_norm_g, heads).reshape(1, d),
      cos_t, sin_t)


def _diff_attn_kernel(lam_init, tk, group, q_ref, k_ref, v_ref, lq1_ref, lk1_ref, lq2_ref,
                      lk2_ref, sg_ref, o_ref, m_ref, acc_ref):
    tq = q_ref.shape[0]
    qi = pl.program_id(2)
    lane = lax.broadcasted_iota(jnp.int32, (tq, LANES), 1)
    q = q_ref[...]
    zero = jnp.zeros_like(q)
    qs = (jnp.where(lane < HEAD, q, zero), jnp.where(lane < HEAD, zero, q))
    lam = (jnp.exp(jnp.sum(lq1_ref[...] * lk1_ref[...], axis=-1, keepdims=True))
           - jnp.exp(jnp.sum(lq2_ref[...] * lk2_ref[...], axis=-1, keepdims=True)) + lam_init)

    def step(start, width, first):
        masked = first
        rows = pl.ds(pl.multiple_of(start, tk), width)
        kb = k_ref[rows, :]
        v_aug = jnp.concatenate([v_ref[rows, :], jnp.ones((width, LANES), BF16)], axis=1)
        s = [lax.dot_general(qs[c], kb, (_NT, ((), ())), preferred_element_type=F32)
             for c in range(2)]
        blocks = [[sc[:, b * LANES:(b + 1) * LANES] for b in range(width // LANES)] for sc in s]
        if masked:
            n_diag = tk // LANES
            qc = lax.broadcasted_iota(jnp.int32, (tq, LANES), 0) // ATTN_CHUNK
            kcol = lax.broadcasted_iota(jnp.int32, (tq, LANES), 1)
            for b in range(n_diag):
                allowed = (kcol + b * LANES) // ATTN_CHUNK <= qc
                for c in range(2):
                    blk = blocks[c][-n_diag + b]
                    blocks[c][-n_diag + b] = jnp.where(allowed, blk, NEG_BIG)
        m_old = [None if first else m_ref[c] for c in range(2)]
        m_new = []
        for c in range(2):
            mb = blocks[c][0]
            for blk in blocks[c][1:]:
                mb = jnp.maximum(mb, blk)
            mx = jnp.broadcast_to(jnp.max(mb, axis=-1, keepdims=True), (tq, LANES))
            m_new.append(mx if first else jnp.maximum(m_old[c], mx))
        p = [jnp.concatenate([jnp.exp2(blk - m_new[c]).astype(BF16) for blk in blocks[c]], axis=1)
             for c in range(2)]
        pv = [jnp.dot(p[c], v_aug, preferred_element_type=F32) for c in range(2)]
        for c in range(2):
            if first:
                acc_ref[c] = pv[c]
            else:
                alpha = jnp.exp2(m_old[c] - m_new[c])
                acc_ref[c] = jnp.concatenate([alpha, alpha], axis=1) * acc_ref[c] + pv[c]
            m_ref[c] = m_new[c]

    n_full = qi
    rem = n_full % group
    for r in range(group):
        @pl.when(rem == r)
        def _():
            step((n_full - r) * tk, (r + 1) * tk, True)

    def group_body(j, carry):
        step(j * (group * tk), group * tk, False)
        return carry

    lax.fori_loop(0, n_full // group, group_body, 0)

    a0 = acc_ref[0]
    a1 = acc_ref[1]
    o = a0[:, :LANES] / a0[:, LANES:] - lam * (a1[:, :LANES] / a1[:, LANES:])
    o = _rms(o, sg_ref[...], SUBLN_EPS) * (1.0 - lam_init)
    o_ref[...] = o.astype(o_ref.dtype)


def _diff_attn(q, k, v, lam_q1, lam_k1, lam_q2, lam_k2, subln_g, lam_init, batch, seq_len,
               tq=512, tk=512, group=6):
    t, d = q.shape
    assert seq_len % tq == 0 and tq == tk and tk % ATTN_CHUNK == 0
    n_q = seq_len // tq
    qtile = pl.BlockSpec((tq, LANES), lambda b, h, i: (b * n_q + i, h))
    kvfull = pl.BlockSpec((seq_len, LANES), lambda b, h, i: (b, h))
    small = lambda n: pl.BlockSpec((1, n), lambda b, h, i: (0, 0))
    return pl.pallas_call(
        functools.partial(_diff_attn_kernel, lam_init, tk, group),
        grid=(batch, d // LANES, n_q),
        in_specs=[qtile, kvfull, kvfull, small(HEAD), small(HEAD), small(HEAD), small(HEAD),
                  small(LANES)],
        out_specs=qtile,
        out_shape=jax.ShapeDtypeStruct((t, d), BF16),
        scratch_shapes=[pltpu.VMEM((2, tq, LANES), F32), pltpu.VMEM((2, tq, 2 * LANES), F32)],
        compiler_params=_cparams(("parallel", "parallel", "arbitrary")),
        name="diff_attn",
    )(q, k, v, lam_q1.reshape(1, HEAD), lam_k1.reshape(1, HEAD), lam_q2.reshape(1, HEAD),
      lam_k2.reshape(1, HEAD), subln_g.reshape(1, LANES))


def kernel(x, g_mix, g_ffn, rw_mu, rw_w_r, rw_w_k, rw_w_v, rw_w_o, rw_w0, rw_w1, rw_w2, rw_a0,
           rw_a1, rw_a2, rw_g1, rw_g2, rw_k_k, rw_k_a, rw_r_k, rw_lnx_w, rw_lnx_b, kv_g, kv_w,
           k_norm_g, da_w_q, da_q_norm_g, da_lam_q1, da_lam_k1, da_lam_q2, da_lam_k2,
           da_subln_g, da_w_o, ffn_w_gu, ffn_w_down):
    batch, seq_len, d = x.shape
    depth = g_mix.shape[0]
    n_rwkv = rw_mu.shape[0]
    x2 = x.reshape(batch * seq_len, d)
    k_sh = v_sh = None
    for layer in range(depth):
        if layer < n_rwkv:
            i = layer
            r, lw, k, v, kkn, a, g = _rwkv_pre(
                x2, seq_len, g_mix[layer], rw_mu[i], rw_w_r[i], rw_w_k[i], rw_w_v[i], rw_w0[i],
                rw_w1[i], rw_w2[i], rw_a0[i], rw_a1[i], rw_a2[i], rw_g1[i], rw_g2[i],
                rw_k_k[i], rw_k_a[i])
            yg = _rwkv_scan(r, lw, k, v, kkn, a, g, rw_r_k[i], rw_lnx_w[i], rw_lnx_b[i],
                            batch, seq_len)
            x2 = _proj_ffn(x2, yg, rw_w_o[i], g_ffn[layer], ffn_w_gu[layer],
                           ffn_w_down[layer], "proj_ffn_rwkv")
        else:
            j = layer - n_rwkv
            lam_init = 0.8 - 0.6 * math.exp(-0.3 * layer)
            q, k_new, v_new = _qkv(x2, seq_len, kv_g, g_mix[layer], kv_w, da_w_q[j],
                                   k_norm_g, da_q_norm_g[j])
            if j == 0:
                k_sh, v_sh = k_new, v_new
            o = _diff_attn(q, k_sh, v_sh, da_lam_q1[j], da_lam_k1[j], da_lam_q2[j],
                           da_lam_k2[j], da_subln_g[j], lam_init, batch, seq_len)
            x2 = _proj_ffn(x2, o, da_w_o[j], g_ffn[layer], ffn_w_gu[layer],
                           ffn_w_down[layer], "proj_ffn_attn")
    return x2.reshape(batch, seq_len, d)
```

```python
import functools
import math

import jax
import jax.numpy as jnp
from jax import lax
from jax.experimental import pallas as pl
from jax.experimental.pallas import tpu as pltpu

F32 = jnp.float32
BF16 = jnp.bfloat16

LANES = 128
HEAD = 64
SCAN_CHUNK = 64
ATTN_CHUNK = 64
ROPE_DIM = 16
ROPE_THETA = 500000.0
NORM_EPS = 1e-6
LNX_EPS = 64e-5
SUBLN_EPS = 1e-5
NEG_BIG = -1e30
VMEM_LIMIT = 56 * 1024 * 1024


def _cparams(sem):
    return pltpu.CompilerParams(dimension_semantics=sem, vmem_limit_bytes=VMEM_LIMIT)


def _rms(x, g, eps=NORM_EPS):
    return x * lax.rsqrt(jnp.mean(x * x, axis=-1, keepdims=True) + eps) * g


def _bdot(a, b):
    return jnp.dot(a.astype(BF16), b.astype(BF16), preferred_element_type=F32)


def _group_matrix(n, group):
    r = lax.broadcasted_iota(jnp.int32, (n, n), 0) // group
    c = lax.broadcasted_iota(jnp.int32, (n, n), 1) // group
    return (r == c).astype(BF16)


def _group_sum(x, gmat):
    return jnp.dot(x.astype(BF16), gmat, preferred_element_type=F32)


def _rwkv_pre_kernel(seq_len, x_ref, xp_ref, g_ref, mu_ref, wr_ref, wk_ref, wv_ref,
                     w0_ref, w1_ref, w2_ref, a0_ref, a1_ref, a2_ref, g1_ref, g2_ref,
                     kk_ref, ka_ref,
                     r_out, lw_out, k_out, v_out, kkn_out, a_out, g_out):
    tm, d = x_ref.shape
    i = pl.program_id(0)
    g = g_ref[...]
    h = _rms(x_ref[...], g)
    hp = _rms(xp_ref[...], g)[7:8, :]
    hp = jnp.where((i * tm) % seq_len == 0, 0.0, hp)
    row = lax.broadcasted_iota(jnp.int32, (tm, d), 0)
    h_prev = jnp.where(row == 0, hp, pltpu.roll(h, 1, axis=0))
    hh = h_prev - h
    mu = mu_ref[...]
    mix = lambda n: h + hh * mu[n:n + 1]

    r_out[...] = _bdot(mix(0), wr_ref[...]).astype(r_out.dtype)
    v_out[...] = _bdot(mix(3), wv_ref[...]).astype(v_out.dtype)
    k = _bdot(mix(2), wk_ref[...])
    wlog = w0_ref[...] + _bdot(jnp.tanh(_bdot(mix(1), w1_ref[...])), w2_ref[...])
    lw_out[...] = -math.exp(-0.5) * jax.nn.sigmoid(wlog)
    a = jax.nn.sigmoid(a0_ref[...] + _bdot(_bdot(mix(4), a1_ref[...]), a2_ref[...]))
    a_out[...] = a.astype(a_out.dtype)
    g_out[...] = _bdot(jax.nn.sigmoid(_bdot(mix(5), g1_ref[...])), g2_ref[...]).astype(g_out.dtype)

    kk = k * kk_ref[...]
    gmat = _group_matrix(LANES, HEAD)
    for c in range(d // LANES):
        sl = slice(c * LANES, (c + 1) * LANES)
        ss = _group_sum(jnp.square(kk[:, sl]), gmat)
        kkn_out[:, sl] = (kk[:, sl] * lax.rsqrt(jnp.maximum(ss, 1e-24))).astype(kkn_out.dtype)
    k_out[...] = (k * (1.0 + (a - 1.0) * ka_ref[...])).astype(k_out.dtype)


def _pad_cols(w, n):
    return jnp.pad(w, ((0, 0), (0, n - w.shape[1])))


def _pad_rows(w, n):
    return jnp.pad(w, ((0, n - w.shape[0]), (0, 0)))


def _rwkv_pre(x2, seq_len, g_mix, mu, w_r, w_k, w_v, w0, w1, w2, a0, a1, a2, g1, g2, k_k, k_a,
              tm=512):
    t, d = x2.shape
    assert seq_len % tm == 0 and t % tm == 0
    lo_w = -(-w1.shape[1] // LANES) * LANES
    lo_a = -(-a1.shape[1] // LANES) * LANES
    lo_g = -(-g1.shape[1] // LANES) * LANES
    w1p, w2p = _pad_cols(w1, lo_w).astype(BF16), _pad_rows(w2, lo_w).astype(BF16)
    a1p, a2p = _pad_cols(a1, lo_a).astype(BF16), _pad_rows(a2, lo_a).astype(BF16)
    g1p, g2p = _pad_cols(g1, lo_g).astype(BF16), _pad_rows(g2, lo_g).astype(BF16)
    row = lambda v: v.reshape(1, d)
    const = lambda shape: pl.BlockSpec(shape, lambda i: (0, 0))
    tile = pl.BlockSpec((tm, d), lambda i: (i, 0))
    in_specs = [
        tile,
        pl.BlockSpec((8, d), lambda i: (jnp.maximum(i * (tm // 8) - 1, 0), 0)),
        const((1, d)), const((6, d)),
        const((d, d)), const((d, d)), const((d, d)),
        const((1, d)), const((d, lo_w)), const((lo_w, d)),
        const((1, d)), const((d, lo_a)), const((lo_a, d)),
        const((d, lo_g)), const((lo_g, d)),
        const((1, d)), const((1, d)),
    ]
    out_shape = [jax.ShapeDtypeStruct((t, d), F32 if n == 1 else BF16) for n in range(7)]
    return pl.pallas_call(
        functools.partial(_rwkv_pre_kernel, seq_len),
        grid=(t // tm,),
        in_specs=in_specs,
        out_specs=[tile] * 7,
        out_shape=out_shape,
        compiler_params=_cparams(("parallel",)),
        name="rwkv_pre",
    )(x2, x2, row(g_mix), mu, w_r.astype(BF16), w_k.astype(BF16), w_v.astype(BF16),
      row(w0), w1p, w2p, row(a0), a1p, a2p, g1p, g2p, row(k_k), row(k_a))


_NN = ((1,), (0,))
_NT = ((1,), (1,))
_TN = ((0,), (0,))


def _mm(a, b, dims=_NN):
    return lax.dot_general(a.astype(BF16), b.astype(BF16), (dims, ((), ())),
                           preferred_element_type=F32)


def _rwkv_scan_kernel(n_s, r_ref, lw_ref, k_ref, v_ref, kk_ref, a_ref, g_ref, rk_ref, lnw_ref,
                      lnb_ref, o_ref, z_ref, rm_ref, fy_ref, gc_ref, bonus_ref, gate_ref, y_ref):
    ts = r_ref.shape[0]
    L = SCAN_CHUNK
    L2 = 2 * L
    n_chunks = ts // L
    step = pl.program_id(0)
    cur = step % 2
    prev = 1 - cur

    @pl.when(step % n_s == 1)
    def _():
        z_ref[...] = jnp.zeros_like(z_ref)

    @pl.when(step == 0)
    def _():
        z_ref[...] = jnp.zeros_like(z_ref)
        rm_ref[1] = jnp.zeros(rm_ref.shape[1:], F32)
        fy_ref[1] = jnp.zeros(fy_ref.shape[1:], F32)
        gc_ref[1] = jnp.zeros(gc_ref.shape[1:], F32)
        bonus_ref[1] = jnp.zeros(bonus_ref.shape[1:], F32)
        gate_ref[1] = jnp.zeros(gate_ref.shape[1:], F32)

    head0 = lax.broadcasted_iota(jnp.int32, (L, LANES), 1) < HEAD
    r2 = lax.broadcasted_iota(jnp.int32, (L2, 2 * L2), 0)
    c2 = lax.broadcasted_iota(jnp.int32, (L2, 2 * L2), 1) % L2
    same = (r2 // L) == (c2 // L)
    m_incl = same & (c2 <= r2)
    m_strict = same & (c2 < r2)
    eye = (lax.broadcasted_iota(jnp.int32, (L2, L2), 0)
           == lax.broadcasted_iota(jnp.int32, (L2, L2), 1))
    eye_f = eye.astype(F32)
    zeros = jnp.zeros((L2, L2), F32)

    def stack(x):
        return jnp.concatenate([jnp.where(head0, x, 0.0), jnp.where(head0, 0.0, x)], axis=0)

    cs = range(n_chunks)
    sls = [slice(c * L, (c + 1) * L) for c in cs]

    state = [z_ref[...]]
    pending = list(cs)

    n_slots = 8
    slots_used = [0]

    def fold_slot():
        slots_used[0] += 1
        done_target = -(-n_chunks * slots_used[0] // n_slots)
        while pending and n_chunks - len(pending) < done_target:
            c = pending.pop(0)
            O = _mm(rm_ref[prev, c], state[0])
            Y = O[:L2] + fy_ref[prev, c]
            state[0] = O[L2:] + gc_ref[prev, c]
            y_ref[sls[c], :] = Y[:L] + Y[L:]

    fold_slot()
    pos = lax.broadcasted_iota(jnp.int32, (L, LANES), 0)
    wl, At, Rt, V, Bt, Kt, AA, RR = ([] for _ in range(8))
    for c in cs:
        sl = sls[c]
        lw = lw_ref[sl, :]
        cum = lw
        shift = 1
        while shift < L:
            cum = cum + jnp.where(pos >= shift, pltpu.roll(cum, shift, axis=0), 0.0)
            shift *= 2
        e_pos = jnp.exp(cum)
        e_neg = jnp.exp(-cum)
        kk = kk_ref[sl, :].astype(F32)
        b = kk * a_ref[sl, :].astype(F32)
        wl.append(e_pos[L - 1:L, :])
        At.append(stack(-kk * jnp.exp(cum - lw)))
        Rt.append(stack(r_ref[sl, :].astype(F32) * e_pos))
        V.append(stack(v_ref[sl, :].astype(F32)))
        Bt.append(stack(b * e_neg))
        Kt.append(stack(k_ref[sl, :].astype(F32) * e_neg))
        BK = jnp.concatenate([Bt[c], Kt[c]], axis=0)
        AA.append(jnp.where(m_strict, _mm(At[c], BK, _NT), 0.0))
        RR.append(jnp.where(m_incl, _mm(Rt[c], BK, _NT), 0.0))
        if c in (n_chunks // 2 - 1, n_chunks - 1):
            fold_slot()
    AkV = [_mm(AA[c][:, L2:], V[c]) for c in cs]
    X = [AA[c][:, :L2] for c in cs]
    T = [eye_f + X[c] for c in cs]
    X = [_mm(X[c], X[c]) for c in cs]
    fold_slot()
    for _ in range(int(math.log2(L)) - 2):
        XT = [_mm(X[c], jnp.concatenate([X[c], T[c]], axis=1)) for c in cs]
        X = [XT[c][:, :L2] for c in cs]
        T = [T[c] + XT[c][:, L2:] for c in cs]
        fold_slot()
    assert not pending
    z_ref[...] = state[0]
    T = [T[c] + _mm(X[c], T[c]) for c in cs]
    PQ = [_mm(T[c], jnp.concatenate([At[c], AkV[c]], axis=1)) for c in cs]
    rhs = [jnp.concatenate([PQ[c], jnp.concatenate([zeros, V[c]], axis=1)], axis=0) for c in cs]

    gmat = _group_matrix(LANES, HEAD)
    y = y_ref[...]
    mean = _group_sum(y, gmat) * (1.0 / HEAD)
    yc = y - mean
    var = _group_sum(yc * yc, gmat) * (1.0 / HEAD)
    yn = yc * lax.rsqrt(var + LNX_EPS) * lnw_ref[...] + lnb_ref[...]
    o_ref[...] = ((yn + bonus_ref[prev]) * gate_ref[prev]).astype(o_ref.dtype)

    F = [_mm(RR[c], rhs[c]) for c in cs]
    G = [_mm(jnp.concatenate([Bt[c] * wl[c], Kt[c] * wl[c]], axis=0), rhs[c], _TN)
         for c in cs]
    for c in cs:
        rm_ref[cur, c] = jnp.concatenate([Rt[c] + F[c][:, :L2],
                                          jnp.where(eye, wl[c], 0.0) + G[c][:, :L2]], axis=0)
        fy_ref[cur, c] = F[c][:, L2:]
        gc_ref[cur, c] = G[c][:, L2:]
    rk = r_ref[...].astype(F32) * k_ref[...].astype(F32) * rk_ref[...]
    bonus_ref[cur] = _group_sum(rk, gmat) * v_ref[...].astype(F32)
    gate_ref[cur] = g_ref[...].astype(F32)


def _rwkv_scan(r, lw, k, v, kkn, a, g, r_k, lnx_w, lnx_b, batch, seq_len, ts=1024):
    t, d = r.shape
    assert seq_len % ts == 0 and seq_len >= 2 * ts and ts % SCAN_CHUNK == 0 and d % LANES == 0
    n_s = seq_len // ts
    n_chunks = ts // SCAN_CHUNK
    n_pairs = d // LANES
    n_tiles = batch * n_pairs * n_s

    def tile_block(i):
        b, p, s = i // (n_pairs * n_s), (i // n_s) % n_pairs, i % n_s
        return b * n_s + s, p

    in_tile = pl.BlockSpec((ts, LANES), lambda i: tile_block(jnp.minimum(i, n_tiles - 1)))
    out_tile = pl.BlockSpec((ts, LANES), lambda i: tile_block(jnp.maximum(i - 1, 0)))
    vec_cur = pl.BlockSpec((1, LANES), lambda i: (0, tile_block(jnp.minimum(i, n_tiles - 1))[1]))
    vec_prev = pl.BlockSpec((1, LANES), lambda i: (0, tile_block(jnp.maximum(i - 1, 0))[1]))
    return pl.pallas_call(
        functools.partial(_rwkv_scan_kernel, n_s),
        grid=(n_tiles + 1,),
        in_specs=[in_tile] * 7 + [vec_cur, vec_prev, vec_prev],
        out_specs=out_tile,
        out_shape=jax.ShapeDtypeStruct((t, d), BF16),
        scratch_shapes=[pltpu.VMEM((LANES, LANES), F32),
                        pltpu.VMEM((2, n_chunks, 2 * LANES, LANES), F32),
                        pltpu.VMEM((2, n_chunks, LANES, LANES), F32),
                        pltpu.VMEM((2, n_chunks, LANES, LANES), F32),
                        pltpu.VMEM((2, ts, LANES), F32),
                        pltpu.VMEM((2, ts, LANES), F32),
                        pltpu.VMEM((ts, LANES), F32)],
        compiler_params=_cparams(("arbitrary",)),
        name="rwkv_scan",
    )(r, lw, k, v, kkn, a, g, r_k.reshape(1, d), lnx_w.reshape(1, d), lnx_b.reshape(1, d))


def _proj_ffn_kernel(tf, x_ref, a_ref, wo_ref, gf_ref, wgu_ref, wd_ref, o_ref, act_ref):
    ff = wd_ref.shape[0]
    x1 = x_ref[...] + jnp.dot(a_ref[...], wo_ref[...], preferred_element_type=F32)
    h = _rms(x1, gf_ref[...]).astype(BF16)
    for f in range(ff // tf):
        gate = jnp.dot(h, wgu_ref[:, f * tf:(f + 1) * tf], preferred_element_type=F32)
        up = jnp.dot(h, wgu_ref[:, ff + f * tf:ff + (f + 1) * tf], preferred_element_type=F32)
        act_ref[:, f * tf:(f + 1) * tf] = ((gate * jax.nn.sigmoid(gate)) * up).astype(BF16)
    o_ref[...] = x1 + jnp.dot(act_ref[...], wd_ref[...], preferred_element_type=F32)


def _proj_ffn(x2, a, w_o, g_ffn, w_gu, w_down, name, tm=512, tf=256):
    t, d = x2.shape
    ff = w_down.shape[0]
    assert t % tm == 0 and ff % tf == 0
    tile = pl.BlockSpec((tm, d), lambda i: (i, 0))
    resident = lambda shape: pl.BlockSpec(shape, lambda i: (0, 0), pipeline_mode=pl.Buffered(1))
    return pl.pallas_call(
        functools.partial(_proj_ffn_kernel, tf),
        grid=(t // tm,),
        in_specs=[tile, tile, resident((d, d)), resident((1, d)), resident((d, 2 * ff)),
                  resident((ff, d))],
        out_specs=tile,
        out_shape=jax.ShapeDtypeStruct((t, d), F32),
        scratch_shapes=[pltpu.VMEM((tm, ff), BF16)],
        compiler_params=_cparams(("parallel",)),
        name=name,
    )(x2, a, w_o.astype(BF16), g_ffn.reshape(1, d), w_gu.astype(BF16), w_down.astype(BF16))


def _head_sumsq(x, gmat):
    return [_group_sum(jnp.square(x[:, c * LANES:(c + 1) * LANES]), gmat)
            for c in range(x.shape[1] // LANES)]


def _head_norm_rope(x, sumsq, gain, cos, sin, out_ref, scale):
    d = x.shape[1]
    lane = lax.broadcasted_iota(jnp.int32, (x.shape[0], LANES), 1) % HEAD
    half = ROPE_DIM // 2
    for c in range(d // LANES):
        sl = slice(c * LANES, (c + 1) * LANES)
        xc = x[:, sl]
        ms = sumsq[c] * (1.0 / HEAD)
        y = xc * lax.rsqrt(ms + NORM_EPS) * gain[:, sl]
        partner = jnp.where(lane < half, pltpu.roll(y, LANES - half, axis=1),
                            pltpu.roll(y, half, axis=1))
        out = y * cos + partner * sin
        if scale != 1.0:
            out = out * scale
        out_ref[:, sl] = out.astype(out_ref.dtype)


def _qkv_kernel(scale, x_ref, gkv_ref, gq_ref, wkv_ref, wq_ref, kn_ref, qn_ref, cos_ref, sin_ref,
                q_out, k_out, v_out):
    d = x_ref.shape[1]
    x = x_ref[...]
    inv = lax.rsqrt(jnp.mean(x * x, axis=-1, keepdims=True) + NORM_EPS)
    xn = x * inv
    hkv = (xn * gkv_ref[...]).astype(BF16)
    hq = (xn * gq_ref[...]).astype(BF16)
    gmat = _group_matrix(LANES, HEAD)
    cos = cos_ref[...]
    sin = sin_ref[...]
    k = jnp.dot(hkv, wkv_ref[:, :d], preferred_element_type=F32)
    q = jnp.dot(hq, wq_ref[...], preferred_element_type=F32)
    k_sumsq = _head_sumsq(k, gmat)
    _head_norm_rope(k, k_sumsq, kn_ref[...], cos, sin, k_out, 1.0)
    q_sumsq = _head_sumsq(q, gmat)
    v_out[...] = jnp.dot(hkv, wkv_ref[:, d:], preferred_element_type=F32).astype(v_out.dtype)
    _head_norm_rope(q, q_sumsq, qn_ref[...], cos, sin, q_out, scale)


def _rope_tables(seq_len):
    half = ROPE_DIM // 2
    pos = jnp.arange(seq_len, dtype=F32)
    inv = jnp.power(ROPE_THETA, -jnp.arange(0, ROPE_DIM, 2, dtype=F32) / ROPE_DIM)
    ang = pos[:, None] * inv[None, :]
    cos, sin = jnp.cos(ang), jnp.sin(ang)
    ones = jnp.ones((seq_len, HEAD - ROPE_DIM), F32)
    cos_h = jnp.concatenate([cos, cos, ones], axis=1)
    sin_h = jnp.concatenate([-sin, sin, 0.0 * ones], axis=1)
    reps = LANES // HEAD
    return jnp.tile(cos_h, (1, reps)), jnp.tile(sin_h, (1, reps))


def _qkv(x2, seq_len, kv_g, g_q, kv_w, w_q, k_norm_g, q_norm_g, tm=512):
    t, d = x2.shape
    assert seq_len % tm == 0
    n_s = seq_len // tm
    cos_t, sin_t = _rope_tables(seq_len)
    heads = d // HEAD
    scale = HEAD ** -0.5 * math.log2(math.e)
    tile = pl.BlockSpec((tm, d), lambda i: (i, 0))
    const = lambda shape: pl.BlockSpec(shape, lambda i: (0, 0))
    tab = pl.BlockSpec((tm, LANES), lambda i: (i % n_s, 0))
    return pl.pallas_call(
        functools.partial(_qkv_kernel, scale),
        grid=(t // tm,),
        in_specs=[tile, const((1, d)), const((1, d)), const((d, 2 * d)), const((d, d)),
                  const((1, d)), const((1, d)), tab, tab],
        out_specs=[tile] * 3,
        out_shape=[jax.ShapeDtypeStruct((t, d), BF16)] * 3,
        compiler_params=_cparams(("parallel",)),
        name="qkv",
    )(x2, kv_g.reshape(1, d), g_q.reshape(1, d), kv_w.astype(BF16), w_q.astype(BF16),
      jnp.tile(k_norm_g, heads).reshape(1, d), jnp.tile(q_norm_g, heads).reshape(1, d),
      cos_t, sin_t)


def _diff_attn_kernel(lam_init, tk, group, q_ref, k_ref, v_ref, lq1_ref, lk1_ref, lq2_ref,
                      lk2_ref, sg_ref, o_ref, m_ref, acc_ref):
    tq = q_ref.shape[0]
    qi = pl.program_id(2)
    lane = lax.broadcasted_iota(jnp.int32, (tq, LANES), 1)
    q = q_ref[...]
    zero = jnp.zeros_like(q)
    qs = (jnp.where(lane < HEAD, q, zero), jnp.where(lane < HEAD, zero, q))
    lam = (jnp.exp(jnp.sum(lq1_ref[...] * lk1_ref[...], axis=-1, keepdims=True))
           - jnp.exp(jnp.sum(lq2_ref[...] * lk2_ref[...], axis=-1, keepdims=True)) + lam_init)

    def step(start, width, first):
        rows = pl.ds(pl.multiple_of(start, tk), width)
        kb = k_ref[rows, :]
        v_aug = jnp.concatenate([v_ref[rows, :], jnp.ones((width, LANES), BF16)], axis=1)
        half = tq // 2
        if first:
            units = [(c, r0, half, width - half + r0) for c in range(2) for r0 in (0, half)]
        else:
            units = [(c, 0, tq, width) for c in range(2)]
        s = [lax.dot_general(qs[c][r0:r0 + n], kb[:kw], (_NT, ((), ())),
                             preferred_element_type=F32) for c, r0, n, kw in units]
        blocks = [[su[:, b * LANES:(b + 1) * LANES] for b in range(kw // LANES)]
                  for su, (_, _, _, kw) in zip(s, units)]
        if first:
            n_diag = half // LANES
            qc = lax.broadcasted_iota(jnp.int32, (half, LANES), 0) // ATTN_CHUNK
            kcol = lax.broadcasted_iota(jnp.int32, (half, LANES), 1)
            for b in range(n_diag):
                allowed = (kcol + b * LANES) // ATTN_CHUNK <= qc
                for u in range(len(units)):
                    blocks[u][-n_diag + b] = jnp.where(allowed, blocks[u][-n_diag + b], NEG_BIG)
        m_old = [None if first else m_ref[c, r0:r0 + n] for c, r0, n, _ in units]
        m_new = []
        for u, (c, r0, n, kw) in enumerate(units):
            mb = blocks[u][0]
            for blk in blocks[u][1:]:
                mb = jnp.maximum(mb, blk)
            mx = jnp.broadcast_to(jnp.max(mb, axis=-1, keepdims=True), (n, LANES))
            m_new.append(mx if first else jnp.maximum(m_old[u], mx))
        p = [jnp.concatenate([jnp.exp2(blk - m_new[u]).astype(BF16) for blk in blocks[u]], axis=1)
             for u in range(len(units))]
        pv = [jnp.dot(p[u], v_aug[:kw], preferred_element_type=F32)
              for u, (_, _, _, kw) in enumerate(units)]
        for u, (c, r0, n, kw) in enumerate(units):
            if first:
                acc_ref[c, r0:r0 + n] = pv[u]
            else:
                alpha = jnp.exp2(m_old[u] - m_new[u])
                acc_ref[c, r0:r0 + n] = (jnp.concatenate([alpha, alpha], axis=1)
                                         * acc_ref[c, r0:r0 + n] + pv[u])
            m_ref[c, r0:r0 + n] = m_new[u]

    n_full = qi
    rem = n_full % group
    for r in range(group):
        @pl.when(rem == r)
        def _():
            step((n_full - r) * tk, (r + 1) * tk, True)

    def group_body(j, carry):
        step(j * (group * tk), group * tk, False)
        return carry

    lax.fori_loop(0, n_full // group, group_body, 0)

    a0 = acc_ref[0]
    a1 = acc_ref[1]
    o = a0[:, :LANES] / a0[:, LANES:] - lam * (a1[:, :LANES] / a1[:, LANES:])
    o = _rms(o, sg_ref[...], SUBLN_EPS) * (1.0 - lam_init)
    o_ref[...] = o.astype(o_ref.dtype)


def _diff_attn(q, k, v, lam_q1, lam_k1, lam_q2, lam_k2, subln_g, lam_init, batch, seq_len,
               tq=512, tk=512, group=6):
    t, d = q.shape
    assert seq_len % tq == 0 and tq == tk and tk % ATTN_CHUNK == 0
    n_q = seq_len // tq
    qtile = pl.BlockSpec((tq, LANES), lambda b, h, i: (b * n_q + i, h))
    kvfull = pl.BlockSpec((seq_len, LANES), lambda b, h, i: (b, h))
    small = lambda n: pl.BlockSpec((1, n), lambda b, h, i: (0, 0))
    return pl.pallas_call(
        functools.partial(_diff_attn_kernel, lam_init, tk, group),
        grid=(batch, d // LANES, n_q),
        in_specs=[qtile, kvfull, kvfull, small(HEAD), small(HEAD), small(HEAD), small(HEAD),
                  small(LANES)],
        out_specs=qtile,
        out_shape=jax.ShapeDtypeStruct((t, d), BF16),
        scratch_shapes=[pltpu.VMEM((2, tq, LANES), F32), pltpu.VMEM((2, tq, 2 * LANES), F32)],
        compiler_params=_cparams(("parallel", "parallel", "arbitrary")),
        name="diff_attn",
    )(q, k, v, lam_q1.reshape(1, HEAD), lam_k1.reshape(1, HEAD), lam_q2.reshape(1, HEAD),
      lam_k2.reshape(1, HEAD), subln_g.reshape(1, LANES))


def kernel(x, g_mix, g_ffn, rw_mu, rw_w_r, rw_w_k, rw_w_v, rw_w_o, rw_w0, rw_w1, rw_w2, rw_a0,
           rw_a1, rw_a2, rw_g1, rw_g2, rw_k_k, rw_k_a, rw_r_k, rw_lnx_w, rw_lnx_b, kv_g, kv_w,
           k_norm_g, da_w_q, da_q_norm_g, da_lam_q1, da_lam_k1, da_lam_q2, da_lam_k2,
           da_subln_g, da_w_o, ffn_w_gu, ffn_w_down):
    batch, seq_len, d = x.shape
    depth = g_mix.shape[0]
    n_rwkv = rw_mu.shape[0]
    x2 = x.reshape(batch * seq_len, d)
    k_sh = v_sh = None
    for layer in range(depth):
        if layer < n_rwkv:
            i = layer
            r, lw, k, v, kkn, a, g = _rwkv_pre(
                x2, seq_len, g_mix[layer], rw_mu[i], rw_w_r[i], rw_w_k[i], rw_w_v[i], rw_w0[i],
                rw_w1[i], rw_w2[i], rw_a0[i], rw_a1[i], rw_a2[i], rw_g1[i], rw_g2[i],
                rw_k_k[i], rw_k_a[i])
            yg = _rwkv_scan(r, lw, k, v, kkn, a, g, rw_r_k[i], rw_lnx_w[i], rw_lnx_b[i],
                            batch, seq_len)
            x2 = _proj_ffn(x2, yg, rw_w_o[i], g_ffn[layer], ffn_w_gu[layer],
                           ffn_w_down[layer], "proj_ffn_rwkv")
        else:
            j = layer - n_rwkv
            lam_init = 0.8 - 0.6 * math.exp(-0.3 * layer)
            q, k_new, v_new = _qkv(x2, seq_len, kv_g, g_mix[layer], kv_w, da_w_q[j],
                                   k_norm_g, da_q_norm_g[j])
            if j == 0:
                k_sh, v_sh = k_new, v_new
            o = _diff_attn(q, k_sh, v_sh, da_lam_q1[j], da_lam_k1[j], da_lam_q2[j],
                           da_lam_k2[j], da_subln_g[j], lam_init, batch, seq_len)
            x2 = _proj_ffn(x2, o, da_w_o[j], g_ffn[layer], ffn_w_gu[layer],
                           ffn_w_down[layer], "proj_ffn_attn")
    return x2.reshape(batch, seq_len, d)
```

```python
import functools
import math

import jax
import jax.numpy as jnp
from jax import lax
from jax.experimental import pallas as pl
from jax.experimental.pallas import tpu as pltpu

F32 = jnp.float32
BF16 = jnp.bfloat16

LANES = 128
HEAD = 64
SCAN_CHUNK = 64
ATTN_CHUNK = 64
ROPE_DIM = 16
ROPE_THETA = 500000.0
NORM_EPS = 1e-6
LNX_EPS = 64e-5
SUBLN_EPS = 1e-5
NEG_BIG = -1e30
VMEM_LIMIT = 56 * 1024 * 1024


def _cparams(sem):
    return pltpu.CompilerParams(dimension_semantics=sem, vmem_limit_bytes=VMEM_LIMIT)


def _rms(x, g, eps=NORM_EPS):
    return x * lax.rsqrt(jnp.mean(x * x, axis=-1, keepdims=True) + eps) * g


def _bdot(a, b):
    return jnp.dot(a.astype(BF16), b.astype(BF16), preferred_element_type=F32)


def _group_matrix(n, group):
    r = lax.broadcasted_iota(jnp.int32, (n, n), 0) // group
    c = lax.broadcasted_iota(jnp.int32, (n, n), 1) // group
    return (r == c).astype(BF16)


def _group_sum(x, gmat):
    return jnp.dot(x.astype(BF16), gmat, preferred_element_type=F32)


def _rwkv_pre_kernel(seq_len, x_ref, xp_ref, g_ref, mu_ref, wr_ref, wk_ref, wv_ref,
                     w0_ref, w1_ref, w2_ref, a0_ref, a1_ref, a2_ref, g1_ref, g2_ref,
                     kk_ref, ka_ref,
                     r_out, lw_out, k_out, v_out, kkn_out, a_out, g_out):
    tm, d = x_ref.shape
    i = pl.program_id(0)
    g = g_ref[...]
    h = _rms(x_ref[...], g)
    hp = _rms(xp_ref[...], g)[7:8, :]
    hp = jnp.where((i * tm) % seq_len == 0, 0.0, hp)
    row = lax.broadcasted_iota(jnp.int32, (tm, d), 0)
    h_prev = jnp.where(row == 0, hp, pltpu.roll(h, 1, axis=0))
    hh = h_prev - h
    mu = mu_ref[...]
    mix = lambda n: h + hh * mu[n:n + 1]

    r_out[...] = _bdot(mix(0), wr_ref[...]).astype(r_out.dtype)
    v_out[...] = _bdot(mix(3), wv_ref[...]).astype(v_out.dtype)
    k = _bdot(mix(2), wk_ref[...])
    wlog = w0_ref[...] + _bdot(jnp.tanh(_bdot(mix(1), w1_ref[...])), w2_ref[...])
    lw_out[...] = -math.exp(-0.5) * jax.nn.sigmoid(wlog)
    a = jax.nn.sigmoid(a0_ref[...] + _bdot(_bdot(mix(4), a1_ref[...]), a2_ref[...]))
    a_out[...] = a.astype(a_out.dtype)
    g_out[...] = _bdot(jax.nn.sigmoid(_bdot(mix(5), g1_ref[...])), g2_ref[...]).astype(g_out.dtype)

    kk = k * kk_ref[...]
    gmat = _group_matrix(LANES, HEAD)
    for c in range(d // LANES):
        sl = slice(c * LANES, (c + 1) * LANES)
        ss = _group_sum(jnp.square(kk[:, sl]), gmat)
        kkn_out[:, sl] = (kk[:, sl] * lax.rsqrt(jnp.maximum(ss, 1e-24))).astype(kkn_out.dtype)
    k_out[...] = (k * (1.0 + (a - 1.0) * ka_ref[...])).astype(k_out.dtype)


def _pad_cols(w, n):
    return jnp.pad(w, ((0, 0), (0, n - w.shape[1])))


def _pad_rows(w, n):
    return jnp.pad(w, ((0, n - w.shape[0]), (0, 0)))


def _rwkv_pre(x2, seq_len, g_mix, mu, w_r, w_k, w_v, w0, w1, w2, a0, a1, a2, g1, g2, k_k, k_a,
              tm=512):
    t, d = x2.shape
    assert seq_len % tm == 0 and t % tm == 0
    lo_w = -(-w1.shape[1] // LANES) * LANES
    lo_a = -(-a1.shape[1] // LANES) * LANES
    lo_g = -(-g1.shape[1] // LANES) * LANES
    w1p, w2p = _pad_cols(w1, lo_w).astype(BF16), _pad_rows(w2, lo_w).astype(BF16)
    a1p, a2p = _pad_cols(a1, lo_a).astype(BF16), _pad_rows(a2, lo_a).astype(BF16)
    g1p, g2p = _pad_cols(g1, lo_g).astype(BF16), _pad_rows(g2, lo_g).astype(BF16)
    row = lambda v: v.reshape(1, d)
    const = lambda shape: pl.BlockSpec(shape, lambda i: (0, 0))
    tile = pl.BlockSpec((tm, d), lambda i: (i, 0))
    in_specs = [
        tile,
        pl.BlockSpec((8, d), lambda i: (jnp.maximum(i * (tm // 8) - 1, 0), 0)),
        const((1, d)), const((6, d)),
        const((d, d)), const((d, d)), const((d, d)),
        const((1, d)), const((d, lo_w)), const((lo_w, d)),
        const((1, d)), const((d, lo_a)), const((lo_a, d)),
        const((d, lo_g)), const((lo_g, d)),
        const((1, d)), const((1, d)),
    ]
    out_shape = [jax.ShapeDtypeStruct((t, d), F32 if n == 1 else BF16) for n in range(7)]
    return pl.pallas_call(
        functools.partial(_rwkv_pre_kernel, seq_len),
        grid=(t // tm,),
        in_specs=in_specs,
        out_specs=[tile] * 7,
        out_shape=out_shape,
        compiler_params=_cparams(("parallel",)),
        name="rwkv_pre",
    )(x2, x2, row(g_mix), mu, w_r.astype(BF16), w_k.astype(BF16), w_v.astype(BF16),
      row(w0), w1p, w2p, row(a0), a1p, a2p, g1p, g2p, row(k_k), row(k_a))


_NN = ((1,), (0,))
_NT = ((1,), (1,))
_TN = ((0,), (0,))


def _mm(a, b, dims=_NN):
    return lax.dot_general(a.astype(BF16), b.astype(BF16), (dims, ((), ())),
                           preferred_element_type=F32)


def _rwkv_scan_kernel(n_s, r_ref, lw_ref, k_ref, v_ref, kk_ref, a_ref, g_ref, rk_ref, lnw_ref,
                      lnb_ref, o_ref, z_ref, rm_ref, fy_ref, gc_ref, bonus_ref, gate_ref, y_ref):
    ts = r_ref.shape[0]
    L = SCAN_CHUNK
    L2 = 2 * L
    n_chunks = ts // L
    step = pl.program_id(0)
    cur = step % 2
    prev = 1 - cur

    @pl.when(step % n_s == 1)
    def _():
        z_ref[...] = jnp.zeros_like(z_ref)

    @pl.when(step == 0)
    def _():
        z_ref[...] = jnp.zeros_like(z_ref)
        rm_ref[1] = jnp.zeros(rm_ref.shape[1:], F32)
        fy_ref[1] = jnp.zeros(fy_ref.shape[1:], F32)
        gc_ref[1] = jnp.zeros(gc_ref.shape[1:], F32)
        bonus_ref[1] = jnp.zeros(bonus_ref.shape[1:], F32)
        gate_ref[1] = jnp.zeros(gate_ref.shape[1:], F32)

    head0 = lax.broadcasted_iota(jnp.int32, (L, LANES), 1) < HEAD
    r2 = lax.broadcasted_iota(jnp.int32, (L2, 2 * L2), 0)
    c2 = lax.broadcasted_iota(jnp.int32, (L2, 2 * L2), 1) % L2
    same = (r2 // L) == (c2 // L)
    m_incl = same & (c2 <= r2)
    m_strict = same & (c2 < r2)
    eye = (lax.broadcasted_iota(jnp.int32, (L2, L2), 0)
           == lax.broadcasted_iota(jnp.int32, (L2, L2), 1))
    eye_f = eye.astype(F32)
    zeros = jnp.zeros((L2, L2), F32)

    def stack(x):
        return jnp.concatenate([jnp.where(head0, x, 0.0), jnp.where(head0, 0.0, x)], axis=0)

    cs = range(n_chunks)
    sls = [slice(c * L, (c + 1) * L) for c in cs]

    state = [z_ref[...]]
    pending = list(cs)

    n_slots = 8
    slots_used = [0]

    def fold_slot():
        slots_used[0] += 1
        done_target = -(-n_chunks * slots_used[0] // n_slots)
        while pending and n_chunks - len(pending) < done_target:
            c = pending.pop(0)
            O = _mm(rm_ref[prev, c], state[0])
            Y = O[:L2] + fy_ref[prev, c]
            state[0] = O[L2:] + gc_ref[prev, c]
            y_ref[sls[c], :] = Y[:L] + Y[L:]

    fold_slot()
    pos = lax.broadcasted_iota(jnp.int32, (L, LANES), 0)
    wl, At, Rt, V, Bt, Kt, AA, RR = ([] for _ in range(8))
    for c in cs:
        sl = sls[c]
        lw = lw_ref[sl, :]
        cum = lw
        shift = 1
        while shift < L:
            cum = cum + jnp.where(pos >= shift, pltpu.roll(cum, shift, axis=0), 0.0)
            shift *= 2
        e_pos = jnp.exp(cum)
        e_neg = jnp.exp(-cum)
        kk = kk_ref[sl, :].astype(F32)
        b = kk * a_ref[sl, :].astype(F32)
        wl.append(e_pos[L - 1:L, :])
        At.append(stack(-kk * jnp.exp(cum - lw)))
        Rt.append(stack(r_ref[sl, :].astype(F32) * e_pos))
        V.append(stack(v_ref[sl, :].astype(F32)))
        Bt.append(stack(b * e_neg))
        Kt.append(stack(k_ref[sl, :].astype(F32) * e_neg))
        BK = jnp.concatenate([Bt[c], Kt[c]], axis=0)
        AA.append(jnp.where(m_strict, _mm(At[c], BK, _NT), 0.0))
        RR.append(jnp.where(m_incl, _mm(Rt[c], BK, _NT), 0.0))
        if c in (n_chunks // 2 - 1, n_chunks - 1):
            fold_slot()
    AkV = [_mm(AA[c][:, L2:], V[c]) for c in cs]
    X = [AA[c][:, :L2] for c in cs]
    T = [eye_f + X[c] for c in cs]
    X = [_mm(X[c], X[c]) for c in cs]
    fold_slot()
    for _ in range(int(math.log2(L)) - 2):
        XT = [_mm(X[c], jnp.concatenate([X[c], T[c]], axis=1)) for c in cs]
        X = [XT[c][:, :L2] for c in cs]
        T = [T[c] + XT[c][:, L2:] for c in cs]
        fold_slot()
    assert not pending
    z_ref[...] = state[0]
    T = [T[c] + _mm(X[c], T[c]) for c in cs]
    PQ = [_mm(T[c], jnp.concatenate([At[c], AkV[c]], axis=1)) for c in cs]
    rhs = [jnp.concatenate([PQ[c], jnp.concatenate([zeros, V[c]], axis=1)], axis=0) for c in cs]

    gmat = _group_matrix(LANES, HEAD)
    y = y_ref[...]
    mean = _group_sum(y, gmat) * (1.0 / HEAD)
    yc = y - mean
    var = _group_sum(yc * yc, gmat) * (1.0 / HEAD)
    yn = yc * lax.rsqrt(var + LNX_EPS) * lnw_ref[...] + lnb_ref[...]
    o_ref[...] = ((yn + bonus_ref[prev]) * gate_ref[prev]).astype(o_ref.dtype)

    F = [_mm(RR[c], rhs[c]) for c in cs]
    G = [_mm(jnp.concatenate([Bt[c] * wl[c], Kt[c] * wl[c]], axis=0), rhs[c], _TN)
         for c in cs]
    for c in cs:
        rm_ref[cur, c] = jnp.concatenate([Rt[c] + F[c][:, :L2],
                                          jnp.where(eye, wl[c], 0.0) + G[c][:, :L2]], axis=0)
        fy_ref[cur, c] = F[c][:, L2:]
        gc_ref[cur, c] = G[c][:, L2:]
    rk = r_ref[...].astype(F32) * k_ref[...].astype(F32) * rk_ref[...]
    bonus_ref[cur] = _group_sum(rk, gmat) * v_ref[...].astype(F32)
    gate_ref[cur] = g_ref[...].astype(F32)


def _rwkv_scan(r, lw, k, v, kkn, a, g, r_k, lnx_w, lnx_b, batch, seq_len, ts=1024):
    t, d = r.shape
    assert seq_len % ts == 0 and seq_len >= 2 * ts and ts % SCAN_CHUNK == 0 and d % LANES == 0
    n_s = seq_len // ts
    n_chunks = ts // SCAN_CHUNK
    n_pairs = d // LANES
    n_tiles = batch * n_pairs * n_s

    def tile_block(i):
        b, p, s = i // (n_pairs * n_s), (i // n_s) % n_pairs, i % n_s
        return b * n_s + s, p

    in_tile = pl.BlockSpec((ts, LANES), lambda i: tile_block(jnp.minimum(i, n_tiles - 1)))
    out_tile = pl.BlockSpec((ts, LANES), lambda i: tile_block(jnp.maximum(i - 1, 0)))
    vec_cur = pl.BlockSpec((1, LANES), lambda i: (0, tile_block(jnp.minimum(i, n_tiles - 1))[1]))
    vec_prev = pl.BlockSpec((1, LANES), lambda i: (0, tile_block(jnp.maximum(i - 1, 0))[1]))
    return pl.pallas_call(
        functools.partial(_rwkv_scan_kernel, n_s),
        grid=(n_tiles + 1,),
        in_specs=[in_tile] * 7 + [vec_cur, vec_prev, vec_prev],
        out_specs=out_tile,
        out_shape=jax.ShapeDtypeStruct((t, d), BF16),
        scratch_shapes=[pltpu.VMEM((LANES, LANES), F32),
                        pltpu.VMEM((2, n_chunks, 2 * LANES, LANES), F32),
                        pltpu.VMEM((2, n_chunks, LANES, LANES), F32),
                        pltpu.VMEM((2, n_chunks, LANES, LANES), F32),
                        pltpu.VMEM((2, ts, LANES), F32),
                        pltpu.VMEM((2, ts, LANES), F32),
                        pltpu.VMEM((ts, LANES), F32)],
        compiler_params=_cparams(("arbitrary",)),
        name="rwkv_scan",
    )(r, lw, k, v, kkn, a, g, r_k.reshape(1, d), lnx_w.reshape(1, d), lnx_b.reshape(1, d))


def _proj_ffn_kernel(tf, x_ref, a_ref, wo_ref, gf_ref, wgu_ref, wd_ref, o_ref, act_ref):
    ff = wd_ref.shape[0]
    x1 = x_ref[...] + jnp.dot(a_ref[...], wo_ref[...], preferred_element_type=F32)
    h = _rms(x1, gf_ref[...]).astype(BF16)
    for f in range(ff // tf):
        gate = jnp.dot(h, wgu_ref[:, f * tf:(f + 1) * tf], preferred_element_type=F32)
        up = jnp.dot(h, wgu_ref[:, ff + f * tf:ff + (f + 1) * tf], preferred_element_type=F32)
        act_ref[:, f * tf:(f + 1) * tf] = ((gate * jax.nn.sigmoid(gate)) * up).astype(BF16)
    o_ref[...] = x1 + jnp.dot(act_ref[...], wd_ref[...], preferred_element_type=F32)


def _proj_ffn(x2, a, w_o, g_ffn, w_gu, w_down, name, tm=512, tf=256):
    t, d = x2.shape
    ff = w_down.shape[0]
    assert t % tm == 0 and ff % tf == 0
    tile = pl.BlockSpec((tm, d), lambda i: (i, 0))
    resident = lambda shape: pl.BlockSpec(shape, lambda i: (0, 0), pipeline_mode=pl.Buffered(1))
    return pl.pallas_call(
        functools.partial(_proj_ffn_kernel, tf),
        grid=(t // tm,),
        in_specs=[tile, tile, resident((d, d)), resident((1, d)), resident((d, 2 * ff)),
                  resident((ff, d))],
        out_specs=tile,
        out_shape=jax.ShapeDtypeStruct((t, d), F32),
        scratch_shapes=[pltpu.VMEM((tm, ff), BF16)],
        compiler_params=_cparams(("parallel",)),
        name=name,
    )(x2, a, w_o.astype(BF16), g_ffn.reshape(1, d), w_gu.astype(BF16), w_down.astype(BF16))


def _head_sumsq(x, gmat):
    return [_group_sum(jnp.square(x[:, c * LANES:(c + 1) * LANES]), gmat)
            for c in range(x.shape[1] // LANES)]


def _head_norm_rope(x, sumsq, gain, cos, sin, out_ref, scale):
    d = x.shape[1]
    lane = lax.broadcasted_iota(jnp.int32, (x.shape[0], LANES), 1) % HEAD
    half = ROPE_DIM // 2
    for c in range(d // LANES):
        sl = slice(c * LANES, (c + 1) * LANES)
        xc = x[:, sl]
        ms = sumsq[c] * (1.0 / HEAD)
        y = xc * lax.rsqrt(ms + NORM_EPS) * gain[:, sl]
        partner = jnp.where(lane < half, pltpu.roll(y, LANES - half, axis=1),
                            pltpu.roll(y, half, axis=1))
        out = y * cos + partner * sin
        if scale != 1.0:
            out = out * scale
        out_ref[:, sl] = out.astype(out_ref.dtype)


def _qkv_kernel(scale, x_ref, gkv_ref, gq_ref, wkv_ref, wq_ref, kn_ref, qn_ref, cos_ref, sin_ref,
                q_out, k_out, v_out):
    d = x_ref.shape[1]
    x = x_ref[...]
    inv = lax.rsqrt(jnp.mean(x * x, axis=-1, keepdims=True) + NORM_EPS)
    xn = x * inv
    hkv = (xn * gkv_ref[...]).astype(BF16)
    hq = (xn * gq_ref[...]).astype(BF16)
    gmat = _group_matrix(LANES, HEAD)
    cos = cos_ref[...]
    sin = sin_ref[...]
    k = jnp.dot(hkv, wkv_ref[:, :d], preferred_element_type=F32)
    q = jnp.dot(hq, wq_ref[...], preferred_element_type=F32)
    k_sumsq = _head_sumsq(k, gmat)
    _head_norm_rope(k, k_sumsq, kn_ref[...], cos, sin, k_out, 1.0)
    q_sumsq = _head_sumsq(q, gmat)
    v_out[...] = jnp.dot(hkv, wkv_ref[:, d:], preferred_element_type=F32).astype(v_out.dtype)
    _head_norm_rope(q, q_sumsq, qn_ref[...], cos, sin, q_out, scale)


def _rope_tables(seq_len):
    half = ROPE_DIM // 2
    pos = jnp.arange(seq_len, dtype=F32)
    inv = jnp.power(ROPE_THETA, -jnp.arange(0, ROPE_DIM, 2, dtype=F32) / ROPE_DIM)
    ang = pos[:, None] * inv[None, :]
    cos, sin = jnp.cos(ang), jnp.sin(ang)
    ones = jnp.ones((seq_len, HEAD - ROPE_DIM), F32)
    cos_h = jnp.concatenate([cos, cos, ones], axis=1)
    sin_h = jnp.concatenate([-sin, sin, 0.0 * ones], axis=1)
    reps = LANES // HEAD
    return jnp.tile(cos_h, (1, reps)), jnp.tile(sin_h, (1, reps))


def _qkv(x2, seq_len, kv_g, g_q, kv_w, w_q, k_norm_g, q_norm_g, tm=512):
    t, d = x2.shape
    assert seq_len % tm == 0
    n_s = seq_len // tm
    cos_t, sin_t = _rope_tables(seq_len)
    heads = d // HEAD
    scale = HEAD ** -0.5 * math.log2(math.e)
    tile = pl.BlockSpec((tm, d), lambda i: (i, 0))
    const = lambda shape: pl.BlockSpec(shape, lambda i: (0, 0))
    tab = pl.BlockSpec((tm, LANES), lambda i: (i % n_s, 0))
    return pl.pallas_call(
        functools.partial(_qkv_kernel, scale),
        grid=(t // tm,),
        in_specs=[tile, const((1, d)), const((1, d)), const((d, 2 * d)), const((d, d)),
                  const((1, d)), const((1, d)), tab, tab],
        out_specs=[tile] * 3,
        out_shape=[jax.ShapeDtypeStruct((t, d), BF16)] * 3,
        compiler_params=_cparams(("parallel",)),
        name="qkv",
    )(x2, kv_g.reshape(1, d), g_q.reshape(1, d), kv_w.astype(BF16), w_q.astype(BF16),
      jnp.tile(k_norm_g, heads).reshape(1, d), jnp.tile(q_norm_g, heads).reshape(1, d),
      cos_t, sin_t)


def _diff_attn_kernel(lam_init, tk, group, q_ref, k_ref, v_ref, lq1_ref, lk1_ref, lq2_ref,
                      lk2_ref, sg_ref, o_ref, m_ref, acc_ref):
    tq = q_ref.shape[0]
    qi = pl.program_id(2)
    lane = lax.broadcasted_iota(jnp.int32, (tq, LANES), 1)
    q = q_ref[...]
    zero = jnp.zeros_like(q)
    qs = (jnp.where(lane < HEAD, q, zero), jnp.where(lane < HEAD, zero, q))
    lam = (jnp.exp(jnp.sum(lq1_ref[...] * lk1_ref[...], axis=-1, keepdims=True))
           - jnp.exp(jnp.sum(lq2_ref[...] * lk2_ref[...], axis=-1, keepdims=True)) + lam_init)

    def step(start, width, first):
        rows = pl.ds(pl.multiple_of(start, tk), width)
        kb = k_ref[rows, :]
        v_aug = jnp.concatenate([v_ref[rows, :], jnp.ones((width, LANES), BF16)], axis=1)
        half = tq // 2
        if first:
            units = [(c, r0, half, width - half + r0) for c in range(2) for r0 in (0, half)]
        else:
            units = [(c, 0, tq, width) for c in range(2)]
        s = [lax.dot_general(qs[c][r0:r0 + n], kb[:kw], (_NT, ((), ())),
                             preferred_element_type=F32) for c, r0, n, kw in units]
        blocks = [[su[:, b * LANES:(b + 1) * LANES] for b in range(kw // LANES)]
                  for su, (_, _, _, kw) in zip(s, units)]
        if first:
            n_diag = half // LANES
            qc = lax.broadcasted_iota(jnp.int32, (half, LANES), 0) // ATTN_CHUNK
            kcol = lax.broadcasted_iota(jnp.int32, (half, LANES), 1)
            for b in range(n_diag):
                allowed = (kcol + b * LANES) // ATTN_CHUNK <= qc
                for u in range(len(units)):
                    blocks[u][-n_diag + b] = jnp.where(allowed, blocks[u][-n_diag + b], NEG_BIG)
        m_old = [None if first else m_ref[c, r0:r0 + n] for c, r0, n, _ in units]
        m_new = []
        for u, (c, r0, n, kw) in enumerate(units):
            mb = blocks[u][0]
            for blk in blocks[u][1:]:
                mb = jnp.maximum(mb, blk)
            mx = jnp.broadcast_to(jnp.max(mb, axis=-1, keepdims=True), (n, LANES))
            m_new.append(mx if first else jnp.maximum(m_old[u], mx))
        p = [jnp.concatenate([jnp.exp2(blk - m_new[u]).astype(BF16) for blk in blocks[u]], axis=1)
             for u in range(len(units))]
        pv = [jnp.dot(p[u], v_aug[:kw], preferred_element_type=F32)
              for u, (_, _, _, kw) in enumerate(units)]
        for u, (c, r0, n, kw) in enumerate(units):
            if first:
                acc_ref[c, r0:r0 + n] = pv[u]
            else:
                alpha = jnp.exp2(m_old[u] - m_new[u])
                acc_ref[c, r0:r0 + n] = (jnp.concatenate([alpha, alpha], axis=1)
                                         * acc_ref[c, r0:r0 + n] + pv[u])
            m_ref[c, r0:r0 + n] = m_new[u]

    n_full = qi
    rem = n_full % group
    for r in range(group):
        @pl.when(rem == r)
        def _():
            step((n_full - r) * tk, (r + 1) * tk, True)

    def group_body(j, carry):
        step(j * (group * tk), group * tk, False)
        return carry

    lax.fori_loop(0, n_full // group, group_body, 0)

    a0 = acc_ref[0]
    a1 = acc_ref[1]
    o = a0[:, :LANES] / a0[:, LANES:] - lam * (a1[:, :LANES] / a1[:, LANES:])
    o = _rms(o, sg_ref[...], SUBLN_EPS) * (1.0 - lam_init)
    o_ref[...] = o.astype(o_ref.dtype)


def _diff_attn(q, k, v, lam_q1, lam_k1, lam_q2, lam_k2, subln_g, lam_init, batch, seq_len,
               tq=1024, tk=1024, group=3):
    t, d = q.shape
    assert seq_len % tq == 0 and tq == tk and tk % ATTN_CHUNK == 0
    n_q = seq_len // tq
    qtile = pl.BlockSpec((tq, LANES), lambda b, h, i: (b * n_q + i, h))
    kvfull = pl.BlockSpec((seq_len, LANES), lambda b, h, i: (b, h))
    small = lambda n: pl.BlockSpec((1, n), lambda b, h, i: (0, 0))
    return pl.pallas_call(
        functools.partial(_diff_attn_kernel, lam_init, tk, group),
        grid=(batch, d // LANES, n_q),
        in_specs=[qtile, kvfull, kvfull, small(HEAD), small(HEAD), small(HEAD), small(HEAD),
                  small(LANES)],
        out_specs=qtile,
        out_shape=jax.ShapeDtypeStruct((t, d), BF16),
        scratch_shapes=[pltpu.VMEM((2, tq, LANES), F32), pltpu.VMEM((2, tq, 2 * LANES), F32)],
        compiler_params=_cparams(("parallel", "parallel", "arbitrary")),
        name="diff_attn",
    )(q, k, v, lam_q1.reshape(1, HEAD), lam_k1.reshape(1, HEAD), lam_q2.reshape(1, HEAD),
      lam_k2.reshape(1, HEAD), subln_g.reshape(1, LANES))


def kernel(x, g_mix, g_ffn, rw_mu, rw_w_r, rw_w_k, rw_w_v, rw_w_o, rw_w0, rw_w1, rw_w2, rw_a0,
           rw_a1, rw_a2, rw_g1, rw_g2, rw_k_k, rw_k_a, rw_r_k, rw_lnx_w, rw_lnx_b, kv_g, kv_w,
           k_norm_g, da_w_q, da_q_norm_g, da_lam_q1, da_lam_k1, da_lam_q2, da_lam_k2,
           da_subln_g, da_w_o, ffn_w_gu, ffn_w_down):
    batch, seq_len, d = x.shape
    depth = g_mix.shape[0]
    n_rwkv = rw_mu.shape[0]
    x2 = x.reshape(batch * seq_len, d)
    k_sh = v_sh = None
    for layer in range(depth):
        if layer < n_rwkv:
            i = layer
            r, lw, k, v, kkn, a, g = _rwkv_pre(
                x2, seq_len, g_mix[layer], rw_mu[i], rw_w_r[i], rw_w_k[i], rw_w_v[i], rw_w0[i],
                rw_w1[i], rw_w2[i], rw_a0[i], rw_a1[i], rw_a2[i], rw_g1[i], rw_g2[i],
                rw_k_k[i], rw_k_a[i])
            yg = _rwkv_scan(r, lw, k, v, kkn, a, g, rw_r_k[i], rw_lnx_w[i], rw_lnx_b[i],
                            batch, seq_len)
            x2 = _proj_ffn(x2, yg, rw_w_o[i], g_ffn[layer], ffn_w_gu[layer],
                           ffn_w_down[layer], "proj_ffn_rwkv")
        else:
            j = layer - n_rwkv
            lam_init = 0.8 - 0.6 * math.exp(-0.3 * layer)
            q, k_new, v_new = _qkv(x2, seq_len, kv_g, g_mix[layer], kv_w, da_w_q[j],
                                   k_norm_g, da_q_norm_g[j])
            if j == 0:
                k_sh, v_sh = k_new, v_new
            o = _diff_attn(q, k_sh, v_sh, da_lam_q1[j], da_lam_k1[j], da_lam_q2[j],
                           da_lam_k2[j], da_subln_g[j], lam_init, batch, seq_len)
            x2 = _proj_ffn(x2, o, da_w_o[j], g_ffn[layer], ffn_w_gu[layer],
                           ffn_w_down[layer], "proj_ffn_attn")
    return x2.reshape(batch, seq_len, d)
```
